```python
import jax, jax.numpy as jnp
from jax import lax
import numpy as np

D_MODEL = 1024
BATCH = 32
SEQ = 2048
DEPTH = 4

CHUNK = 64
Q_BLOCK = 128
PLE_DIM = 256
DEEPNORM_ALPHA = (2 * DEPTH) ** 0.25
DEEPNORM_BETA = (8 * DEPTH) ** -0.25
LN_EPS = 1e-5
RMS_EPS = 1e-6

MIX_WIDTH = D_MODEL
HEAD_DIM = 64
POOL_WIDTH = MIX_WIDTH // 4
POOL_GROUPS = 4
POOL_GROUP_DIM = POOL_WIDTH // POOL_GROUPS
POOL_WINDOWS = (2, 4, 8, 16)
SSD_WIDTH = 3 * MIX_WIDTH // 8
SSD_HEADS = SSD_WIDTH // HEAD_DIM
SSD_GROUPS = 2
SSD_STATE = 128
SSD_CONV = 4
SSD_XBC_WIDTH = SSD_WIDTH + 2 * SSD_GROUPS * SSD_STATE
SB_WIDTH = MIX_WIDTH - POOL_WIDTH - SSD_WIDTH
SB_HEADS = SB_WIDTH // HEAD_DIM

COL_POOL = 0
COL_Z = COL_POOL + POOL_WIDTH
COL_XBC = COL_Z + SSD_WIDTH
COL_DT = COL_XBC + SSD_XBC_WIDTH
COL_Q = COL_DT + SSD_HEADS
COL_K = COL_Q + SB_WIDTH
COL_V = COL_K + SB_WIDTH
IN_COLS = COL_V + SB_WIDTH

D_FF = ((8 * D_MODEL // 3 + 255) // 256) * 256
FFN_CONV = 3

kernel_name = 'hybrid_pool_ssd_stickbreak_deepnorm_trunk'


def layer_norm(x, g, b):
    xf = x.astype(jnp.float32)
    mu = jnp.mean(xf, axis=-1, keepdims=True)
    var = jnp.mean(jnp.square(xf - mu), axis=-1, keepdims=True)
    y = (xf - mu) * lax.rsqrt(var + LN_EPS)
    return (y * g + b).astype(x.dtype)


def causal_depthwise_conv(x, w, b):
    k = w.shape[0]
    s_ = x.shape[1]
    xp = jnp.pad(x, ((0, 0), (k - 1, 0), (0, 0)))
    out = b
    for j in range(k):
        out = out + xp[:, j:j + s_] * w[j]
    return out


def multiscale_pool(u, w_pool, scale):
    b_, s_, _ = u.shape
    ug = u.reshape(b_, s_, POOL_GROUPS, POOL_GROUP_DIM).astype(jnp.float32)
    cs = jnp.concatenate([jnp.zeros_like(ug[:, :1]), jnp.cumsum(ug, axis=1)], axis=1)
    t = jnp.arange(s_)[:, None]
    win = jnp.array(POOL_WINDOWS, dtype=jnp.int32)[None, :]
    lo = jnp.maximum(t + 1 - win, 0)
    lower = cs[:, lo, jnp.arange(POOL_GROUPS)[None, :]]
    count = (t + 1 - lo).astype(jnp.float32)
    pooled = (cs[:, 1:] - lower) / count[..., None] - ug
    mixed = jnp.einsum('bsgc,gcd->bsgd', pooled.astype(u.dtype), w_pool)
    return mixed.reshape(b_, s_, POOL_WIDTH) * scale


def ssd_chunked_scan(xs, dt, a, bm, cm):
    b_, s_, h_, p_ = xs.shape
    c_ = s_ // CHUNK
    r_ = h_ // SSD_GROUPS
    xdt = (xs.astype(jnp.float32) * dt[..., None]).reshape(b_, c_, CHUNK, SSD_GROUPS, r_, p_)
    adt = (dt * a).reshape(b_, c_, CHUNK, SSD_GROUPS, r_)
    a_cum = jnp.cumsum(jnp.moveaxis(adt, 2, -1), axis=-1)
    bc = bm.astype(jnp.float32).reshape(b_, c_, CHUNK, SSD_GROUPS, SSD_STATE)
    cc = cm.astype(jnp.float32).reshape(b_, c_, CHUNK, SSD_GROUPS, SSD_STATE)
    causal = jnp.tril(jnp.ones((CHUNK, CHUNK), dtype=bool))
    seg = a_cum[..., :, None] - a_cum[..., None, :]
    decay = jnp.where(causal, jnp.exp(jnp.where(causal, seg, 0.0)), 0.0)
    cb = jnp.einsum('bclgn,bcsgn->bcgls', cc, bc)
    y_diag = jnp.einsum('bcgrls,bcsgrp->bclgrp', cb[:, :, :, None] * decay, xdt)
    decay_to_end = jnp.exp(a_cum[..., -1:] - a_cum)
    states = jnp.einsum('bclgn,bcgrl,bclgrp->bcgrpn', bc, decay_to_end, xdt)
    chunk_decay = jnp.exp(a_cum[..., -1])

    def carry_state(h, inp):
        st, dec = inp
        return h * dec[..., None, None] + st, h

    h0 = jnp.zeros_like(states[:, 0])
    _, h_prev = lax.scan(carry_state, h0, (jnp.moveaxis(states, 1, 0), jnp.moveaxis(chunk_decay, 1, 0)))
    h_prev = jnp.moveaxis(h_prev, 0, 1)
    y_off = jnp.einsum('bclgn,bcgrpn,bcgrl->bclgrp', cc, h_prev, jnp.exp(a_cum))
    return (y_diag + y_off).reshape(b_, s_, h_, p_)


def ssd_mixer(z, xbc, dt_raw, conv_w, conv_b, dt_bias, a_log, d_skip, norm_w):
    b_, s_, _ = xbc.shape
    xbc = jax.nn.silu(causal_depthwise_conv(xbc, conv_w, conv_b))
    xs = xbc[..., :SSD_WIDTH].reshape(b_, s_, SSD_HEADS, HEAD_DIM)
    bm = xbc[..., SSD_WIDTH:SSD_WIDTH + SSD_GROUPS * SSD_STATE].reshape(b_, s_, SSD_GROUPS, SSD_STATE)
    cm = xbc[..., SSD_WIDTH + SSD_GROUPS * SSD_STATE:].reshape(b_, s_, SSD_GROUPS, SSD_STATE)
    dt = jax.nn.softplus((dt_raw + dt_bias).astype(jnp.float32))
    a = -jnp.exp(a_log.astype(jnp.float32))
    y = ssd_chunked_scan(xs, dt, a, bm, cm) + xs.astype(jnp.float32) * d_skip[:, None]
    hg = (y.reshape(b_, s_, SSD_WIDTH) * jax.nn.silu(z.astype(jnp.float32)))
    hg = hg.reshape(b_, s_, SSD_GROUPS, SSD_WIDTH // SSD_GROUPS)
    hg = hg * lax.rsqrt(jnp.mean(jnp.square(hg), axis=-1, keepdims=True) + RMS_EPS)
    return (hg.reshape(b_, s_, SSD_WIDTH) * norm_w).astype(xbc.dtype)


def stick_breaking_attention(q, k, v):
    b_, s_, h_, d_ = q.shape
    scale = d_ ** -0.5
    outs = []
    for start in range(0, s_, Q_BLOCK):
        end = start + Q_BLOCK
        z = jnp.einsum('bqhd,bkhd->bhqk', q[:, start:end], k[:, :end]).astype(jnp.float32) * scale
        t_idx = start + jnp.arange(Q_BLOCK)[:, None]
        s_idx = jnp.arange(end)[None, :]
        strict = s_idx < t_idx
        log_not = jnp.where(strict, jax.nn.log_sigmoid(-z), 0.0)
        tail = lax.cumsum(log_not, axis=3, reverse=True) - log_not
        weights = jnp.where(strict, jnp.exp(jax.nn.log_sigmoid(z) + tail), 0.0)
        outs.append(jnp.einsum('bhqk,bkhd->bqhd', weights.astype(v.dtype), v[:, :end]))
    return jnp.concatenate(outs, axis=1)


def conv_glu_ffn(x, w_up, conv_w, conv_b, w_down):
    up = causal_depthwise_conv(x @ w_up, conv_w, conv_b)
    gate, val = up[..., :D_FF], up[..., D_FF:]
    return (jax.nn.silu(gate) * val) @ w_down


def setup_inputs(seed: int = 0) -> dict:
    key = jax.random.key(seed)
    ks = jax.random.split(key, 26)

    def nrm(k, shape, scale):
        return jax.random.normal(k, shape, jnp.float32) * scale

    dt0 = jnp.exp(jax.random.uniform(ks[6], (DEPTH, SSD_HEADS), jnp.float32, np.log(1e-3), np.log(1e-1)))
    return {
        'x': nrm(ks[0], (BATCH, SEQ, D_MODEL), 1.0),
        'p': nrm(ks[1], (DEPTH, BATCH, SEQ, PLE_DIM), 1.0),
        'w_in': nrm(ks[2], (DEPTH, D_MODEL, IN_COLS), D_MODEL ** -0.5),
        'pool_w': nrm(ks[3], (DEPTH, POOL_GROUPS, POOL_GROUP_DIM, POOL_GROUP_DIM), POOL_GROUP_DIM ** -0.5),
        'pool_scale': 1.0 + nrm(ks[4], (DEPTH, POOL_WIDTH), 0.1),
        'ssd_conv_w': nrm(ks[5], (DEPTH, SSD_CONV, SSD_XBC_WIDTH), SSD_CONV ** -0.5),
        'ssd_conv_b': nrm(ks[7], (DEPTH, SSD_XBC_WIDTH), 0.01),
        'ssd_dt_bias': dt0 + jnp.log(-jnp.expm1(-dt0)),
        'ssd_a_log': jnp.log(jax.random.uniform(ks[8], (DEPTH, SSD_HEADS), jnp.float32, 1.0, 16.0)),
        'ssd_d': 1.0 + nrm(ks[9], (DEPTH, SSD_HEADS), 0.1),
        'ssd_norm_w': 1.0 + nrm(ks[10], (DEPTH, SSD_WIDTH), 0.05),
        'w_out': nrm(ks[11], (DEPTH, MIX_WIDTH, D_MODEL), MIX_WIDTH ** -0.5 * DEEPNORM_BETA),
        'ln1_g': 1.0 + nrm(ks[12], (DEPTH, D_MODEL), 0.05),
        'ln1_b': nrm(ks[13], (DEPTH, D_MODEL), 0.01),
        'ffn_w_up': nrm(ks[14], (DEPTH, D_MODEL, 2 * D_FF), D_MODEL ** -0.5),
        'ffn_conv_w': nrm(ks[15], (DEPTH, FFN_CONV, 2 * D_FF), FFN_CONV ** -0.5),
        'ffn_conv_b': nrm(ks[16], (DEPTH, 2 * D_FF), 0.01),
        'ffn_w_down': nrm(ks[17], (DEPTH, D_FF, D_MODEL), D_FF ** -0.5 * DEEPNORM_BETA),
        'ln2_g': 1.0 + nrm(ks[18], (DEPTH, D_MODEL), 0.05),
        'ln2_b': nrm(ks[19], (DEPTH, D_MODEL), 0.01),
        'ple_w_gate': nrm(ks[20], (DEPTH, D_MODEL, D_MODEL), D_MODEL ** -0.5),
        'ple_w_proj': nrm(ks[21], (DEPTH, PLE_DIM, D_MODEL), PLE_DIM ** -0.5 * DEEPNORM_BETA),
    }


def reference(x, p, w_in, pool_w, pool_scale, ssd_conv_w, ssd_conv_b, ssd_dt_bias, ssd_a_log, ssd_d,
              ssd_norm_w, w_out, ln1_g, ln1_b, ffn_w_up, ffn_conv_w, ffn_conv_b, ffn_w_down, ln2_g, ln2_b,
              ple_w_gate, ple_w_proj):
    b_, s_, _ = x.shape
    for i in range(DEPTH):
        h = x @ w_in[i]
        pool_out = multiscale_pool(h[..., COL_POOL:COL_Z], pool_w[i], pool_scale[i])
        ssd_out = ssd_mixer(h[..., COL_Z:COL_XBC], h[..., COL_XBC:COL_DT], h[..., COL_DT:COL_Q],
                            ssd_conv_w[i], ssd_conv_b[i], ssd_dt_bias[i], ssd_a_log[i], ssd_d[i], ssd_norm_w[i])
        q = h[..., COL_Q:COL_K].reshape(b_, s_, SB_HEADS, HEAD_DIM)
        k = h[..., COL_K:COL_V].reshape(b_, s_, SB_HEADS, HEAD_DIM)
        v = h[..., COL_V:IN_COLS].reshape(b_, s_, SB_HEADS, HEAD_DIM)
        sb_out = stick_breaking_attention(q, k, v).reshape(b_, s_, SB_WIDTH)
        mix = jnp.concatenate([pool_out.astype(x.dtype), ssd_out.astype(x.dtype), sb_out.astype(x.dtype)], axis=-1) @ w_out[i]
        x = layer_norm(DEEPNORM_ALPHA * x + mix, ln1_g[i], ln1_b[i])
        ffn = conv_glu_ffn(x, ffn_w_up[i], ffn_conv_w[i], ffn_conv_b[i], ffn_w_down[i])
        ple = jax.nn.sigmoid(x @ ple_w_gate[i]) * (p[i] @ ple_w_proj[i])
        x = layer_norm(DEEPNORM_ALPHA * x + ffn + ple, ln2_g[i], ln2_b[i])
    return x
```

```python
import functools

import jax
import jax.numpy as jnp
from jax import lax
from jax.experimental import pallas as pl
from jax.experimental.pallas import tpu as pltpu

F32 = jnp.float32
BF16 = jnp.bfloat16

D_MODEL = 1024
DEPTH = 4
PLE_DIM = 256
DEEPNORM_ALPHA = (2 * DEPTH) ** 0.25
LN_EPS = 1e-5
RMS_EPS = 1e-6
HEAD_DIM = 64
POOL_WIDTH = 256
POOL_GROUPS = 4
POOL_GROUP_DIM = 64
POOL_WINDOWS = (2, 4, 8, 16)
SSD_WIDTH = 384
SSD_HEADS = 6
SSD_GROUPS = 2
SSD_GROUP_WIDTH = SSD_WIDTH // SSD_GROUPS
SSD_STATE = 128
SSD_CONV = 4
SSD_XBC_WIDTH = 896
SB_WIDTH = 384
D_FF = 2816
FFN_CONV = 3
COL_Z = 256
COL_XBC = COL_Z + SSD_WIDTH
COL_DT = COL_XBC + SSD_XBC_WIDTH
COL_Q = COL_DT + SSD_HEADS
COL_K = COL_Q + SB_WIDTH
COL_V = COL_K + SB_WIDTH

LANES = 128
SUBLANES = 8
DT_PAD = LANES
VMEM_LIMIT = 56 * 1024 * 1024

IN_TM = 1024
SSD_L = 256
SB_T = 256
TAIL_TM = 512
FF_CHUNK = 256
N_FF_CHUNKS = D_FF // FF_CHUNK

IN_GROUPS = (("pool", POOL_WIDTH, BF16), ("z", SSD_WIDTH, BF16), ("xbc", SSD_XBC_WIDTH, BF16),
             ("dt", DT_PAD, F32), ("q", SB_WIDTH, BF16), ("k", SB_WIDTH, BF16), ("v", SB_WIDTH, BF16))
IN_COLS_PADDED = sum(w for _, w, _ in IN_GROUPS)


def _dot(a, b):
    return jnp.dot(a, b, preferred_element_type=F32)


def _dot_nt(a, b):
    return lax.dot_general(a, b, (((1,), (1,)), ((), ())), preferred_element_type=F32)


def _split3(x):
    hi = x.astype(BF16)
    r1 = x - hi.astype(F32)
    mid = r1.astype(BF16)
    lo = (r1 - mid.astype(F32)).astype(BF16)
    return hi, mid, lo


def _dot_sel_rhs(x, sel):
    hi, mid, lo = _split3(x)
    return _dot(hi, sel) + _dot(mid, sel) + _dot(lo, sel)


def _dot_sel_lhs(sel, x):
    hi, mid, lo = _split3(x)
    return _dot(sel, hi) + _dot(sel, mid) + _dot(sel, lo)


def _sigmoid(x):
    return 1.0 / (1.0 + jnp.exp(-x))


def _softplus(x):
    return jnp.maximum(x, 0.0) + jnp.log(1.0 + jnp.exp(-jnp.abs(x)))


def _layer_norm(y, g, b):
    mu = jnp.mean(y, axis=-1, keepdims=True)
    yc = y - mu
    var = jnp.mean(yc * yc, axis=-1, keepdims=True)
    return yc * lax.rsqrt(var + LN_EPS) * g + b


def _shift_rows(prev_rows, x, d):
    n = x.shape[0]
    cat = jnp.concatenate([prev_rows, x], axis=0)
    return pltpu.roll(cat, d, axis=0)[SUBLANES:SUBLANES + n]


def _const_spec(shape):
    zeros = (0,) * len(shape)
    return pl.BlockSpec(shape, lambda *_: zeros, pipeline_mode=pl.Buffered(1))


def _in_proj_kernel(x_ref, w_ref, *out_refs):
    xb = x_ref[...].astype(BF16)
    col = 0
    for (_, width, dtype), o_ref in zip(IN_GROUPS, out_refs):
        o_ref[...] = _dot(xb, w_ref[:, col:col + width]).astype(dtype)
        col += width


def _in_proj(x2d, w_in_r):
    m = x2d.shape[0]
    return pl.pallas_call(
        _in_proj_kernel,
        grid=(m // IN_TM,),
        in_specs=[pl.BlockSpec((IN_TM, D_MODEL), lambda i: (i, 0)),
                  _const_spec((D_MODEL, IN_COLS_PADDED))],
        out_specs=[pl.BlockSpec((IN_TM, w), lambda i: (i, 0)) for _, w, _ in IN_GROUPS],
        out_shape=[jax.ShapeDtypeStruct((m, w), dt) for _, w, dt in IN_GROUPS],
        compiler_params=pltpu.CompilerParams(dimension_semantics=("parallel",),
                                             vmem_limit_bytes=VMEM_LIMIT),
        name="in_proj",
    )(x2d, w_in_r)


def _pool_kernel(u_ref, w_ref, scale_ref, o_ref):
    u = u_ref[...].astype(F32)
    row = lax.broadcasted_iota(jnp.int32, u.shape, 0)
    lane = lax.broadcasted_iota(jnp.int32, u.shape, 1)

    def shifted(x, d):
        return jnp.where(row >= d, pltpu.roll(x, d, axis=0), 0.0)

    s2 = u + shifted(u, 1)
    s4 = s2 + shifted(s2, 2)
    s8 = s4 + shifted(s4, 4)
    s16 = s8 + shifted(s8, 8)
    group = lane // POOL_GROUP_DIM
    win_sum = jnp.where(group == 0, s2, jnp.where(group == 1, s4, jnp.where(group == 2, s8, s16)))
    win = jnp.where(group == 0, 2, jnp.where(group == 1, 4, jnp.where(group == 2, 8, 16)))
    count = jnp.minimum(row + 1, win).astype(F32)
    pooled = win_sum / count - u
    o_ref[...] = (_dot(pooled.astype(BF16), w_ref[...]) * scale_ref[...]).astype(o_ref.dtype)


def _pool(u2d, w_bd, scale, batch, seq):
    m = u2d.shape[0]
    return pl.pallas_call(
        _pool_kernel,
        grid=(batch,),
        in_specs=[pl.BlockSpec((seq, POOL_WIDTH), lambda b: (b, 0)),
                  _const_spec((POOL_WIDTH, POOL_WIDTH)),
                  _const_spec((1, POOL_WIDTH))],
        out_specs=pl.BlockSpec((seq, POOL_WIDTH), lambda b: (b, 0)),
        out_shape=jax.ShapeDtypeStruct((m, POOL_WIDTH), BF16),
        compiler_params=pltpu.CompilerParams(dimension_semantics=("parallel",),
                                             vmem_limit_bytes=VMEM_LIMIT),
        name="pool_mixer",
    )(u2d, w_bd, scale)


def _ssd_kernel(z_ref, xbc_ref, dt_ref, cw_ref, cb_ref, dtb_ref, alog_ref, dskip_ref, nw_ref,
                o_ref, tail_ref, state_ref):
    L = SSD_L

    @pl.when(pl.program_id(1) == 0)
    def _():
        tail_ref[...] = jnp.zeros_like(tail_ref)
        state_ref[...] = jnp.zeros_like(state_ref)

    x = xbc_ref[...].astype(F32)
    prev = tail_ref[...]
    conv = cb_ref[...] + cw_ref[3:4, :] * x
    for d in (1, 2, 3):
        conv = conv + cw_ref[3 - d:4 - d, :] * _shift_rows(prev, x, d)
    tail_ref[...] = x[L - SUBLANES:, :]
    xc = conv * _sigmoid(conv)
    xs = xc[:, :SSD_WIDTH]

    dt = _softplus(dt_ref[...] + dtb_ref[...])
    adt = dt * (-jnp.exp(alog_ref[...]))
    r_ll = lax.broadcasted_iota(jnp.int32, (L, L), 0)
    c_ll = lax.broadcasted_iota(jnp.int32, (L, L), 1)
    causal = c_ll <= r_ll
    lower = jnp.where(causal, 1.0, 0.0).astype(BF16)
    a_cum = _dot_sel_lhs(lower, adt)
    a_cum_t = a_cum.T

    e_r = lax.broadcasted_iota(jnp.int32, (DT_PAD, SSD_WIDTH), 0)
    e_c = lax.broadcasted_iota(jnp.int32, (DT_PAD, SSD_WIDTH), 1)
    expand = jnp.where(e_c // HEAD_DIM == e_r, 1.0, 0.0).astype(BF16)
    dt_e = _dot_sel_rhs(dt, expand)
    a_cum_e = _dot_sel_rhs(a_cum, expand)
    a_tot_e = a_cum_e[L - 1:L, :]

    xdt = xs * dt_e
    xdt_b = xdt.astype(BF16)
    xw_b = (xdt * jnp.exp(a_tot_e - a_cum_e)).astype(BF16)
    lane_w = lax.broadcasted_iota(jnp.int32, (L, SSD_WIDTH), 1)
    state = state_ref[...]
    state_b = state.astype(BF16)

    heads = []
    y_off = None
    new_state = None
    for g in range(SSD_GROUPS):
        b_off = SSD_WIDTH + g * SSD_STATE
        c_off = SSD_WIDTH + SSD_GROUPS * SSD_STATE + g * SSD_STATE
        bm = xc[:, b_off:b_off + SSD_STATE]
        cm_b = xc[:, c_off:c_off + SSD_STATE].astype(BF16)
        cb = _dot_nt(cm_b, bm.astype(BF16))
        for r in range(SSD_HEADS // SSD_GROUPS):
            h = g * (SSD_HEADS // SSD_GROUPS) + r
            seg = a_cum[:, h:h + 1] - a_cum_t[h:h + 1, :]
            decay = jnp.exp(jnp.where(causal, seg, -1e30))
            wmat = (cb * decay).astype(BF16)
            pair = (h // 2) * LANES
            heads.append(_dot(wmat, xdt_b[:, pair:pair + LANES]))
        st_g = _dot(bm.T.astype(BF16), xw_b)
        yo_g = _dot(cm_b, state_b)
        if g == 0:
            new_state, y_off = st_g, yo_g
        else:
            in_g0 = lax.broadcasted_iota(jnp.int32, st_g.shape, 1) < SSD_GROUP_WIDTH
            new_state = jnp.where(in_g0, new_state, st_g)
            y_off = jnp.where(lane_w < SSD_GROUP_WIDTH, y_off, yo_g)

    state_ref[...] = state * jnp.exp(a_tot_e) + new_state

    lane_p = lax.broadcasted_iota(jnp.int32, (L, LANES), 1)
    y_diag = jnp.concatenate(
        [jnp.where(lane_p < HEAD_DIM, heads[2 * p], heads[2 * p + 1]) for p in range(SSD_HEADS // 2)],
        axis=1)
    y = y_diag + y_off * jnp.exp(a_cum_e) + xs * dskip_ref[...]

    zf = z_ref[...].astype(F32)
    hg = y * (zf * _sigmoid(zf))
    sq = hg * hg
    in_g0 = lane_w < SSD_GROUP_WIDTH
    ms0 = jnp.sum(jnp.where(in_g0, sq, 0.0), axis=-1, keepdims=True) * (1.0 / SSD_GROUP_WIDTH)
    ms1 = jnp.sum(jnp.where(in_g0, 0.0, sq), axis=-1, keepdims=True) * (1.0 / SSD_GROUP_WIDTH)
    inv = jnp.where(in_g0, lax.rsqrt(ms0 + RMS_EPS), lax.rsqrt(ms1 + RMS_EPS))
    o_ref[...] = (hg * inv * nw_ref[...]).astype(o_ref.dtype)


def _ssd(z2d, xbc2d, dt2d, cw, cb, dtb, alog, dskip, nw, batch, seq):
    m = z2d.shape[0]
    nc = seq // SSD_L
    row_map = lambda b, c: (b * nc + c, 0)
    return pl.pallas_call(
        _ssd_kernel,
        grid=(batch, nc),
        in_specs=[pl.BlockSpec((SSD_L, SSD_WIDTH), row_map),
                  pl.BlockSpec((SSD_L, SSD_XBC_WIDTH), row_map),
                  pl.BlockSpec((SSD_L, DT_PAD), row_map),
                  _const_spec((SSD_CONV, SSD_XBC_WIDTH)),
                  _const_spec((1, SSD_XBC_WIDTH)),
                  _const_spec((1, DT_PAD)),
                  _const_spec((1, DT_PAD)),
                  _const_spec((1, SSD_WIDTH)),
                  _const_spec((1, SSD_WIDTH))],
        out_specs=pl.BlockSpec((SSD_L, SSD_WIDTH), row_map),
        out_shape=jax.ShapeDtypeStruct((m, SSD_WIDTH), BF16),
        scratch_shapes=[pltpu.VMEM((SUBLANES, SSD_XBC_WIDTH), F32),
                        pltpu.VMEM((SSD_STATE, SSD_WIDTH), F32)],
        compiler_params=pltpu.CompilerParams(dimension_semantics=("parallel", "arbitrary"),
                                             vmem_limit_bytes=VMEM_LIMIT),
        name="ssd_mixer",
    )(z2d, xbc2d, dt2d, cw, cb, dtb, alog, dskip, nw)


def _sb_kernel(q_ref, k_ref, v_ref, o_ref):
    T = SB_T
    qi = pl.program_id(2)
    lane = lax.broadcasted_iota(jnp.int32, (T, LANES), 1)
    r_tt = lax.broadcasted_iota(jnp.int32, (T, T), 0)
    c_tt = lax.broadcasted_iota(jnp.int32, (T, T), 1)
    strict = c_tt < r_tt
    later = jnp.where(r_tt > c_tt, 1.0, 0.0).astype(BF16)
    qs = q_ref[...] * (HEAD_DIM ** -0.5)

    def tile(j, qh, carry, acc, diagonal):
        start = pl.multiple_of(j * T, T)
        kj = k_ref[pl.ds(start, T), :]
        vj = v_ref[pl.ds(start, T), :]
        z = _dot_nt(qh, kj)
        sp = _softplus(z)
        spm = jnp.where(strict, sp, 0.0) if diagonal else sp
        hi = spm.astype(BF16)
        lo = (spm - hi.astype(F32)).astype(BF16)
        right = _dot(hi, later) + _dot(lo, later)
        w = jnp.exp(z - sp - right - carry)
        if diagonal:
            w = jnp.where(strict, w, 0.0)
        acc = acc + _dot(w.astype(BF16), vj)
        carry = carry + right[:, 0:1] + spm[:, 0:1]
        return carry, acc

    outs = []
    for hh in range(2):
        qh = jnp.where((lane >= HEAD_DIM) == bool(hh), qs, jnp.zeros_like(qs))
        carry0 = jnp.zeros((T, 1), F32)
        acc0 = jnp.zeros((T, LANES), F32)
        carry, acc = tile(qi, qh, carry0, acc0, True)

        def body(it, ca, qh=qh):
            return tile(qi - 1 - it, qh, ca[0], ca[1], False)

        carry, acc = lax.fori_loop(0, qi, body, (carry, acc))
        outs.append(acc)
    o_ref[...] = jnp.where(lane < HEAD_DIM, outs[0], outs[1]).astype(o_ref.dtype)


def _sb_attention(q2d, k2d, v2d, batch, seq):
    m = q2d.shape[0]
    nq = seq // SB_T
    pairs = SB_WIDTH // LANES
    q_map = lambda b, p, i: (b * nq + i, p)
    kv_map = lambda b, p, i: (b, p)
    return pl.pallas_call(
        _sb_kernel,
        grid=(batch, pairs, nq),
        in_specs=[pl.BlockSpec((SB_T, LANES), q_map),
                  pl.BlockSpec((seq, LANES), kv_map),
                  pl.BlockSpec((seq, LANES), kv_map)],
        out_specs=pl.BlockSpec((SB_T, LANES), q_map),
        out_shape=jax.ShapeDtypeStruct((m, SB_WIDTH), BF16),
        compiler_params=pltpu.CompilerParams(
            dimension_semantics=("parallel", "parallel", "arbitrary"),
            vmem_limit_bytes=VMEM_LIMIT),
        name="sb_attention",
    )(q2d, k2d, v2d)


def _tail_kernel(x_ref, pool_ref, ssd_ref, sb_ref, p_ref, wout_ref, g1_ref, b1_ref,
                 wup_ref, fcw_ref, fcb_ref, wdown_ref, wgate_ref, wproj_ref, g2_ref, b2_ref,
                 o_ref, carry_ref, acc_ref):
    tm = TAIL_TM

    @pl.when(pl.program_id(1) == 0)
    def _():
        carry_ref[...] = jnp.zeros_like(carry_ref)

    mix = (_dot(pool_ref[...], wout_ref[0:POOL_WIDTH, :])
           + _dot(ssd_ref[...], wout_ref[POOL_WIDTH:POOL_WIDTH + SSD_WIDTH, :])
           + _dot(sb_ref[...], wout_ref[POOL_WIDTH + SSD_WIDTH:, :]))
    x1 = _layer_norm(DEEPNORM_ALPHA * x_ref[...] + mix, g1_ref[...], b1_ref[...])
    xb = x1.astype(BF16)

    gate = _sigmoid(_dot(xb, wgate_ref[...]))
    acc_ref[...] = DEEPNORM_ALPHA * x1 + gate * _dot(p_ref[...].astype(BF16), wproj_ref[...])

    for c in range(N_FF_CHUNKS):
        up = _dot(xb, wup_ref[c])
        prev = carry_ref[c]
        conv = (fcb_ref[c] + fcw_ref[c, 2:3, :] * up
                + fcw_ref[c, 1:2, :] * _shift_rows(prev, up, 1)
                + fcw_ref[c, 0:1, :] * _shift_rows(prev, up, 2))
        carry_ref[c] = up[tm - SUBLANES:, :]
        gl = conv[:, :FF_CHUNK]
        act = (gl * _sigmoid(gl) * conv[:, FF_CHUNK:]).astype(BF16)
        acc_ref[...] += _dot(act, wdown_ref[c])

    o_ref[...] = _layer_norm(acc_ref[...], g2_ref[...], b2_ref[...])


def _tail(x2d, pool_o, ssd_o, sb_o, p2d, wout, g1, b1, wup, fcw, fcb, wdown, wgate, wproj, g2, b2,
          batch, seq):
    m = x2d.shape[0]
    nt = seq // TAIL_TM
    row_map = lambda b, i: (b * nt + i, 0)
    return pl.pallas_call(
        _tail_kernel,
        grid=(batch, nt),
        in_specs=[pl.BlockSpec((TAIL_TM, D_MODEL), row_map),
                  pl.BlockSpec((TAIL_TM, POOL_WIDTH), row_map),
                  pl.BlockSpec((TAIL_TM, SSD_WIDTH), row_map),
                  pl.BlockSpec((TAIL_TM, SB_WIDTH), row_map),
                  pl.BlockSpec((TAIL_TM, PLE_DIM), row_map),
                  _const_spec((D_MODEL, D_MODEL)),
                  _const_spec((1, D_MODEL)),
                  _const_spec((1, D_MODEL)),
                  _const_spec((N_FF_CHUNKS, D_MODEL, 2 * FF_CHUNK)),
                  _const_spec((N_FF_CHUNKS, FFN_CONV, 2 * FF_CHUNK)),
                  _const_spec((N_FF_CHUNKS, 1, 2 * FF_CHUNK)),
                  _const_spec((N_FF_CHUNKS, FF_CHUNK, D_MODEL)),
                  _const_spec((D_MODEL, D_MODEL)),
                  _const_spec((PLE_DIM, D_MODEL)),
                  _const_spec((1, D_MODEL)),
                  _const_spec((1, D_MODEL))],
        out_specs=pl.BlockSpec((TAIL_TM, D_MODEL), row_map),
        out_shape=jax.ShapeDtypeStruct((m, D_MODEL), F32),
        scratch_shapes=[pltpu.VMEM((N_FF_CHUNKS, SUBLANES, 2 * FF_CHUNK), F32),
                        pltpu.VMEM((TAIL_TM, D_MODEL), F32)],
        compiler_params=pltpu.CompilerParams(dimension_semantics=("parallel", "arbitrary"),
                                             vmem_limit_bytes=VMEM_LIMIT),
        name="layer_tail",
    )(x2d, pool_o, ssd_o, sb_o, p2d, wout, g1, b1, wup, fcw, fcb, wdown, wgate, wproj, g2, b2)


def _chunk_cols(a):
    lead = a.shape[:-1]
    g = a[..., :D_FF].reshape(*lead, N_FF_CHUNKS, FF_CHUNK)
    v = a[..., D_FF:].reshape(*lead, N_FF_CHUNKS, FF_CHUNK)
    gv = jnp.concatenate([g, v], axis=-1)
    return jnp.moveaxis(gv, -2, 0)


def kernel(x, p, w_in, pool_w, pool_scale, ssd_conv_w, ssd_conv_b, ssd_dt_bias, ssd_a_log, ssd_d,
           ssd_norm_w, w_out, ln1_g, ln1_b, ffn_w_up, ffn_conv_w, ffn_conv_b, ffn_w_down, ln2_g, ln2_b,
           ple_w_gate, ple_w_proj):
    batch, seq, _ = x.shape
    m = batch * seq
    x2d = x.reshape(m, D_MODEL)
    pad6 = lambda a: jnp.pad(a, (0, DT_PAD - SSD_HEADS)).reshape(1, DT_PAD)
    for i in range(DEPTH):
        wi = w_in[i]
        w_in_r = jnp.concatenate(
            [wi[:, :COL_DT], jnp.pad(wi[:, COL_DT:COL_Q], ((0, 0), (0, DT_PAD - SSD_HEADS))), wi[:, COL_Q:]],
            axis=1).astype(BF16)
        u, z, xbc, dt, q, k, v = _in_proj(x2d, w_in_r)

        w_bd = jax.scipy.linalg.block_diag(*[pool_w[i, g] for g in range(POOL_GROUPS)]).astype(BF16)
        pool_o = _pool(u, w_bd, pool_scale[i].reshape(1, POOL_WIDTH), batch, seq)

        ssd_o = _ssd(z, xbc, dt, ssd_conv_w[i], ssd_conv_b[i].reshape(1, -1), pad6(ssd_dt_bias[i]),
                     pad6(ssd_a_log[i]), jnp.repeat(ssd_d[i], HEAD_DIM).reshape(1, SSD_WIDTH),
                     ssd_norm_w[i].reshape(1, SSD_WIDTH), batch, seq)

        sb_o = _sb_attention(q, k, v, batch, seq)

        x2d = _tail(x2d, pool_o, ssd_o, sb_o, p[i].reshape(m, PLE_DIM),
                    w_out[i].astype(BF16), ln1_g[i].reshape(1, -1), ln1_b[i].reshape(1, -1),
                    _chunk_cols(ffn_w_up[i]).astype(BF16), _chunk_cols(ffn_conv_w[i]),
                    _chunk_cols(ffn_conv_b[i].reshape(1, -1)),
                    ffn_w_down[i].reshape(N_FF_CHUNKS, FF_CHUNK, D_MODEL).astype(BF16),
                    ple_w_gate[i].astype(BF16), ple_w_proj[i].astype(BF16),
                    ln2_g[i].reshape(1, -1), ln2_b[i].reshape(1, -1), batch, seq)
    return x2d.reshape(batch, seq, D_MODEL)
```

```python
import functools

import jax
import jax.numpy as jnp
from jax import lax
from jax.experimental import pallas as pl
from jax.experimental.pallas import tpu as pltpu

F32 = jnp.float32
BF16 = jnp.bfloat16

D_MODEL = 1024
DEPTH = 4
PLE_DIM = 256
DEEPNORM_ALPHA = (2 * DEPTH) ** 0.25
LN_EPS = 1e-5
RMS_EPS = 1e-6
HEAD_DIM = 64
POOL_WIDTH = 256
POOL_GROUPS = 4
POOL_GROUP_DIM = 64
POOL_WINDOWS = (2, 4, 8, 16)
SSD_WIDTH = 384
SSD_HEADS = 6
SSD_GROUPS = 2
SSD_GROUP_WIDTH = SSD_WIDTH // SSD_GROUPS
SSD_STATE = 128
SSD_CONV = 4
SSD_XBC_WIDTH = 896
SB_WIDTH = 384
D_FF = 2816
FFN_CONV = 3
COL_Z = 256
COL_XBC = COL_Z + SSD_WIDTH
COL_DT = COL_XBC + SSD_XBC_WIDTH
COL_Q = COL_DT + SSD_HEADS
COL_K = COL_Q + SB_WIDTH
COL_V = COL_K + SB_WIDTH

LANES = 128
SUBLANES = 8
DT_PAD = LANES
VMEM_LIMIT = 56 * 1024 * 1024

IN_TM = 1024
SSD_L = 256
SB_T = 256
SB_UNDERFLOW = 105.0
TAIL_TM = 256
FF_CHUNK = 256
N_FF_CHUNKS = D_FF // FF_CHUNK

IN_GROUPS = (("pool", POOL_WIDTH, BF16), ("z", SSD_WIDTH, BF16), ("xbc", SSD_XBC_WIDTH, BF16),
             ("dt", DT_PAD, F32), ("q", SB_WIDTH, BF16), ("k", SB_WIDTH, BF16), ("v", SB_WIDTH, BF16))
IN_COLS_PADDED = sum(w for _, w, _ in IN_GROUPS)


def _dot(a, b):
    return jnp.dot(a, b, preferred_element_type=F32)


def _dot_nt(a, b):
    return lax.dot_general(a, b, (((1,), (1,)), ((), ())), preferred_element_type=F32)


def _split3(x):
    hi = x.astype(BF16)
    r1 = x - hi.astype(F32)
    mid = r1.astype(BF16)
    lo = (r1 - mid.astype(F32)).astype(BF16)
    return hi, mid, lo


def _dot_sel_rhs(x, sel):
    hi, mid, lo = _split3(x)
    return _dot(hi, sel) + _dot(mid, sel) + _dot(lo, sel)


def _dot_sel_lhs(sel, x):
    hi, mid, lo = _split3(x)
    return _dot(sel, hi) + _dot(sel, mid) + _dot(sel, lo)


def _sigmoid(x):
    return 1.0 / (1.0 + jnp.exp(-x))


def _softplus(x):
    return jnp.maximum(x, 0.0) + jnp.log(1.0 + jnp.exp(-jnp.abs(x)))


def _layer_norm(y, g, b):
    mu = jnp.mean(y, axis=-1, keepdims=True)
    yc = y - mu
    var = jnp.mean(yc * yc, axis=-1, keepdims=True)
    return yc * lax.rsqrt(var + LN_EPS) * g + b


def _shift_rows(prev_rows, x, d):
    n = x.shape[0]
    cat = jnp.concatenate([prev_rows, x], axis=0)
    return pltpu.roll(cat, d, axis=0)[SUBLANES:SUBLANES + n]


def _const_spec(shape):
    zeros = (0,) * len(shape)
    return pl.BlockSpec(shape, lambda *_: zeros, pipeline_mode=pl.Buffered(1))


def _in_proj_kernel(x_ref, w_ref, *out_refs):
    xb = x_ref[...].astype(BF16)
    col = 0
    for (_, width, dtype), o_ref in zip(IN_GROUPS, out_refs):
        o_ref[...] = _dot(xb, w_ref[:, col:col + width]).astype(dtype)
        col += width


def _in_proj(x2d, w_in_r):
    m = x2d.shape[0]
    return pl.pallas_call(
        _in_proj_kernel,
        grid=(m // IN_TM,),
        in_specs=[pl.BlockSpec((IN_TM, D_MODEL), lambda i: (i, 0)),
                  _const_spec((D_MODEL, IN_COLS_PADDED))],
        out_specs=[pl.BlockSpec((IN_TM, w), lambda i: (i, 0)) for _, w, _ in IN_GROUPS],
        out_shape=[jax.ShapeDtypeStruct((m, w), dt) for _, w, dt in IN_GROUPS],
        compiler_params=pltpu.CompilerParams(dimension_semantics=("parallel",),
                                             vmem_limit_bytes=VMEM_LIMIT),
        name="in_proj",
    )(x2d, w_in_r)


def _pool_kernel(u_ref, w_ref, scale_ref, o_ref):
    u = u_ref[...].astype(F32)
    row = lax.broadcasted_iota(jnp.int32, u.shape, 0)
    lane = lax.broadcasted_iota(jnp.int32, u.shape, 1)

    def shifted(x, d):
        return jnp.where(row >= d, pltpu.roll(x, d, axis=0), 0.0)

    s2 = u + shifted(u, 1)
    s4 = s2 + shifted(s2, 2)
    s8 = s4 + shifted(s4, 4)
    s16 = s8 + shifted(s8, 8)
    group = lane // POOL_GROUP_DIM
    win_sum = jnp.where(group == 0, s2, jnp.where(group == 1, s4, jnp.where(group == 2, s8, s16)))
    win = jnp.where(group == 0, 2, jnp.where(group == 1, 4, jnp.where(group == 2, 8, 16)))
    count = jnp.minimum(row + 1, win).astype(F32)
    pooled = win_sum / count - u
    o_ref[...] = (_dot(pooled.astype(BF16), w_ref[...]) * scale_ref[...]).astype(o_ref.dtype)


def _pool(u2d, w_bd, scale, batch, seq):
    m = u2d.shape[0]
    return pl.pallas_call(
        _pool_kernel,
        grid=(batch,),
        in_specs=[pl.BlockSpec((seq, POOL_WIDTH), lambda b: (b, 0)),
                  _const_spec((POOL_WIDTH, POOL_WIDTH)),
                  _const_spec((1, POOL_WIDTH))],
        out_specs=pl.BlockSpec((seq, POOL_WIDTH), lambda b: (b, 0)),
        out_shape=jax.ShapeDtypeStruct((m, POOL_WIDTH), BF16),
        compiler_params=pltpu.CompilerParams(dimension_semantics=("parallel",),
                                             vmem_limit_bytes=VMEM_LIMIT),
        name="pool_mixer",
    )(u2d, w_bd, scale)


def _ssd_kernel(z_ref, xbc_ref, dt_ref, cw_ref, cb_ref, dtb_ref, alog_ref, dskip_ref, nw_ref,
                o_ref, tail_ref, state_ref):
    L = SSD_L

    @pl.when(pl.program_id(1) == 0)
    def _():
        tail_ref[...] = jnp.zeros_like(tail_ref)
        state_ref[...] = jnp.zeros_like(state_ref)

    x = xbc_ref[...].astype(F32)
    prev = tail_ref[...]
    conv = cb_ref[...] + cw_ref[3:4, :] * x
    for d in (1, 2, 3):
        conv = conv + cw_ref[3 - d:4 - d, :] * _shift_rows(prev, x, d)
    tail_ref[...] = x[L - SUBLANES:, :]
    xc = conv * _sigmoid(conv)
    xs = xc[:, :SSD_WIDTH]

    dt = _softplus(dt_ref[...] + dtb_ref[...])
    adt = dt * (-jnp.exp(alog_ref[...]))
    r_ll = lax.broadcasted_iota(jnp.int32, (L, L), 0)
    c_ll = lax.broadcasted_iota(jnp.int32, (L, L), 1)
    causal = c_ll <= r_ll
    lower = jnp.where(causal, 1.0, 0.0).astype(BF16)
    a_cum = _dot_sel_lhs(lower, adt)
    a_cum_t = a_cum.T

    e_r = lax.broadcasted_iota(jnp.int32, (DT_PAD, SSD_WIDTH), 0)
    e_c = lax.broadcasted_iota(jnp.int32, (DT_PAD, SSD_WIDTH), 1)
    expand = jnp.where(e_c // HEAD_DIM == e_r, 1.0, 0.0).astype(BF16)
    dt_e = _dot_sel_rhs(dt, expand)
    a_cum_e = _dot_sel_rhs(a_cum, expand)
    a_tot_e = a_cum_e[L - 1:L, :]

    xdt = xs * dt_e
    xdt_b = xdt.astype(BF16)
    xw_b = (xdt * jnp.exp(a_tot_e - a_cum_e)).astype(BF16)
    lane_w = lax.broadcasted_iota(jnp.int32, (L, SSD_WIDTH), 1)
    state = state_ref[...]
    state_b = state.astype(BF16)

    heads = []
    y_off = None
    new_state = None
    for g in range(SSD_GROUPS):
        b_off = SSD_WIDTH + g * SSD_STATE
        c_off = SSD_WIDTH + SSD_GROUPS * SSD_STATE + g * SSD_STATE
        bm = xc[:, b_off:b_off + SSD_STATE]
        cm_b = xc[:, c_off:c_off + SSD_STATE].astype(BF16)
        cb = _dot_nt(cm_b, bm.astype(BF16))
        for r in range(SSD_HEADS // SSD_GROUPS):
            h = g * (SSD_HEADS // SSD_GROUPS) + r
            seg = a_cum[:, h:h + 1] - a_cum_t[h:h + 1, :]
            decay = jnp.exp(jnp.where(causal, seg, -1e30))
            wmat = (cb * decay).astype(BF16)
            pair = (h // 2) * LANES
            heads.append(_dot(wmat, xdt_b[:, pair:pair + LANES]))
        st_g = _dot(bm.T.astype(BF16), xw_b)
        yo_g = _dot(cm_b, state_b)
        if g == 0:
            new_state, y_off = st_g, yo_g
        else:
            in_g0 = lax.broadcasted_iota(jnp.int32, st_g.shape, 1) < SSD_GROUP_WIDTH
            new_state = jnp.where(in_g0, new_state, st_g)
            y_off = jnp.where(lane_w < SSD_GROUP_WIDTH, y_off, yo_g)

    state_ref[...] = state * jnp.exp(a_tot_e) + new_state

    lane_p = lax.broadcasted_iota(jnp.int32, (L, LANES), 1)
    y_diag = jnp.concatenate(
        [jnp.where(lane_p < HEAD_DIM, heads[2 * p], heads[2 * p + 1]) for p in range(SSD_HEADS // 2)],
        axis=1)
    y = y_diag + y_off * jnp.exp(a_cum_e) + xs * dskip_ref[...]

    zf = z_ref[...].astype(F32)
    hg = y * (zf * _sigmoid(zf))
    sq = hg * hg
    in_g0 = lane_w < SSD_GROUP_WIDTH
    ms0 = jnp.sum(jnp.where(in_g0, sq, 0.0), axis=-1, keepdims=True) * (1.0 / SSD_GROUP_WIDTH)
    ms1 = jnp.sum(jnp.where(in_g0, 0.0, sq), axis=-1, keepdims=True) * (1.0 / SSD_GROUP_WIDTH)
    inv = jnp.where(in_g0, lax.rsqrt(ms0 + RMS_EPS), lax.rsqrt(ms1 + RMS_EPS))
    o_ref[...] = (hg * inv * nw_ref[...]).astype(o_ref.dtype)


def _ssd(z2d, xbc2d, dt2d, cw, cb, dtb, alog, dskip, nw, batch, seq):
    m = z2d.shape[0]
    nc = seq // SSD_L
    row_map = lambda b, c: (b * nc + c, 0)
    return pl.pallas_call(
        _ssd_kernel,
        grid=(batch, nc),
        in_specs=[pl.BlockSpec((SSD_L, SSD_WIDTH), row_map),
                  pl.BlockSpec((SSD_L, SSD_XBC_WIDTH), row_map),
                  pl.BlockSpec((SSD_L, DT_PAD), row_map),
                  _const_spec((SSD_CONV, SSD_XBC_WIDTH)),
                  _const_spec((1, SSD_XBC_WIDTH)),
                  _const_spec((1, DT_PAD)),
                  _const_spec((1, DT_PAD)),
                  _const_spec((1, SSD_WIDTH)),
                  _const_spec((1, SSD_WIDTH))],
        out_specs=pl.BlockSpec((SSD_L, SSD_WIDTH), row_map),
        out_shape=jax.ShapeDtypeStruct((m, SSD_WIDTH), BF16),
        scratch_shapes=[pltpu.VMEM((SUBLANES, SSD_XBC_WIDTH), F32),
                        pltpu.VMEM((SSD_STATE, SSD_WIDTH), F32)],
        compiler_params=pltpu.CompilerParams(dimension_semantics=("parallel", "arbitrary"),
                                             vmem_limit_bytes=VMEM_LIMIT),
        name="ssd_mixer",
    )(z2d, xbc2d, dt2d, cw, cb, dtb, alog, dskip, nw)


def _sb_kernel(q_ref, k_ref, v_ref, o_ref):
    T = SB_T
    nq = q_ref.shape[0] // T
    lane = lax.broadcasted_iota(jnp.int32, (T, LANES), 1)
    r_tt = lax.broadcasted_iota(jnp.int32, (T, T), 0)
    c_tt = lax.broadcasted_iota(jnp.int32, (T, T), 1)
    strict = c_tt < r_tt
    later = jnp.where(r_tt > c_tt, 1.0, 0.0).astype(BF16)
    head_lanes = [(lane >= HEAD_DIM) == bool(hh) for hh in range(2)]

    def rows(ref, j):
        return ref[pl.ds(pl.multiple_of(j * T, T), T), :]

    def head_queries(qi):
        qs = rows(q_ref, qi) * (HEAD_DIM ** -0.5)
        return [jnp.where(m, qs, jnp.zeros_like(qs)) for m in head_lanes]

    def tile_terms(z, diagonal):
        sp = _softplus(z)
        spm = jnp.where(strict, sp, 0.0) if diagonal else sp
        hi = spm.astype(BF16)
        lo = (spm - hi.astype(F32)).astype(BF16)
        right = _dot(hi, later) + _dot(lo, later)
        return z - sp, right, right[:, 0:1] + spm[:, 0:1]

    def tile_out(logsig, right, carry, vj, diagonal):
        w = jnp.exp(logsig - right if carry is None else logsig - right - carry)
        if diagonal:
            w = jnp.where(strict, w, 0.0)
        return _dot(w.astype(BF16), vj)

    def store(qi, accs):
        o_ref[pl.ds(pl.multiple_of(qi * T, T), T), :] = jnp.where(
            lane < HEAD_DIM, accs[0], accs[1]).astype(o_ref.dtype)

    zs = [_dot_nt(qh, rows(k_ref, 0)) for qh in head_queries(0)]
    terms = [tile_terms(z, True) for z in zs]
    store(0, [tile_out(ls, right, None, rows(v_ref, 0), True) for ls, right, _ in terms])

    def q_tile(qi, _):
        qhs = head_queries(qi)
        kd, vd = rows(k_ref, qi), rows(v_ref, qi)
        ko, vo = rows(k_ref, qi - 1), rows(v_ref, qi - 1)
        zs = [(_dot_nt(qh, kd), _dot_nt(qh, ko)) for qh in qhs]
        terms = [(tile_terms(zd, True), tile_terms(zo, False)) for zd, zo in zs]
        accs, carries = [], []
        for (ls_d, right_d, tot_d), (ls_o, right_o, tot_o) in terms:
            accs.append(tile_out(ls_d, right_d, None, vd, True) + tile_out(ls_o, right_o, tot_d, vo, False))
            carries.append(tot_d + tot_o)

        def cond(st):
            return jnp.logical_and(st[0] >= 0, st[1] < SB_UNDERFLOW)

        def body(st):
            j, _, c0, a0, c1, a1 = st
            kj, vj = rows(k_ref, j), rows(v_ref, j)
            terms = [tile_terms(z, False) for z in [_dot_nt(qh, kj) for qh in qhs]]
            new = [(c + tot, a + tile_out(ls, right, c, vj, False))
                   for (ls, right, tot), c, a in zip(terms, (c0, c1), (a0, a1))]
            cmin = jnp.min(jnp.minimum(new[0][0], new[1][0]))
            return j - 1, cmin, new[0][0], new[0][1], new[1][0], new[1][1]

        cmin0 = jnp.min(jnp.minimum(carries[0], carries[1]))
        st = lax.while_loop(cond, body, (qi - 2, cmin0, carries[0], accs[0], carries[1], accs[1]))
        store(qi, (st[3], st[5]))
        return 0

    lax.fori_loop(1, nq, q_tile, 0)


def _sb_attention(q2d, k2d, v2d, batch, seq):
    m = q2d.shape[0]
    pairs = SB_WIDTH // LANES
    blk = pl.BlockSpec((seq, LANES), lambda b, p: (b, p))
    return pl.pallas_call(
        _sb_kernel,
        grid=(batch, pairs),
        in_specs=[blk, blk, blk],
        out_specs=blk,
        out_shape=jax.ShapeDtypeStruct((m, SB_WIDTH), BF16),
        compiler_params=pltpu.CompilerParams(dimension_semantics=("parallel", "parallel"),
                                             vmem_limit_bytes=VMEM_LIMIT),
        name="sb_attention",
    )(q2d, k2d, v2d)


def _tail_kernel(x_ref, pool_ref, ssd_ref, sb_ref, p_ref, wout_ref, g1_ref, b1_ref,
                 wup_ref, fcw_ref, fcb_ref, wdown_ref, wgate_ref, wproj_ref, g2_ref, b2_ref,
                 o_ref, carry_ref, acc_ref):
    tm = TAIL_TM

    @pl.when(pl.program_id(1) == 0)
    def _():
        carry_ref[...] = jnp.zeros_like(carry_ref)

    mix = (_dot(pool_ref[...], wout_ref[0:POOL_WIDTH, :])
           + _dot(ssd_ref[...], wout_ref[POOL_WIDTH:POOL_WIDTH + SSD_WIDTH, :])
           + _dot(sb_ref[...], wout_ref[POOL_WIDTH + SSD_WIDTH:, :]))
    x1 = _layer_norm(DEEPNORM_ALPHA * x_ref[...] + mix, g1_ref[...], b1_ref[...])
    xb = x1.astype(BF16)

    gate = _sigmoid(_dot(xb, wgate_ref[...]))
    acc_ref[...] = DEEPNORM_ALPHA * x1 + gate * _dot(p_ref[...].astype(BF16), wproj_ref[...])

    next_up = _dot(xb, wup_ref[0])
    for c in range(N_FF_CHUNKS):
        up = next_up
        if c + 1 < N_FF_CHUNKS:
            next_up = _dot(xb, wup_ref[c + 1])
        prev = carry_ref[c]
        conv = (fcb_ref[c] + fcw_ref[c, 2:3, :] * up
                + fcw_ref[c, 1:2, :] * _shift_rows(prev, up, 1)
                + fcw_ref[c, 0:1, :] * _shift_rows(prev, up, 2))
        carry_ref[c] = up[tm - SUBLANES:, :]
        gl = conv[:, :FF_CHUNK]
        act = (gl * _sigmoid(gl) * conv[:, FF_CHUNK:]).astype(BF16)
        acc_ref[...] += _dot(act, wdown_ref[c])

    o_ref[...] = _layer_norm(acc_ref[...], g2_ref[...], b2_ref[...])


def _tail(x2d, pool_o, ssd_o, sb_o, p2d, wout, g1, b1, wup, fcw, fcb, wdown, wgate, wproj, g2, b2,
          batch, seq):
    m = x2d.shape[0]
    nt = seq // TAIL_TM
    row_map = lambda b, i: (b * nt + i, 0)
    return pl.pallas_call(
        _tail_kernel,
        grid=(batch, nt),
        in_specs=[pl.BlockSpec((TAIL_TM, D_MODEL), row_map),
                  pl.BlockSpec((TAIL_TM, POOL_WIDTH), row_map),
                  pl.BlockSpec((TAIL_TM, SSD_WIDTH), row_map),
                  pl.BlockSpec((TAIL_TM, SB_WIDTH), row_map),
                  pl.BlockSpec((TAIL_TM, PLE_DIM), row_map),
                  _const_spec((D_MODEL, D_MODEL)),
                  _const_spec((1, D_MODEL)),
                  _const_spec((1, D_MODEL)),
                  _const_spec((N_FF_CHUNKS, D_MODEL, 2 * FF_CHUNK)),
                  _const_spec((N_FF_CHUNKS, FFN_CONV, 2 * FF_CHUNK)),
                  _const_spec((N_FF_CHUNKS, 1, 2 * FF_CHUNK)),
                  _const_spec((N_FF_CHUNKS, FF_CHUNK, D_MODEL)),
                  _const_spec((D_MODEL, D_MODEL)),
                  _const_spec((PLE_DIM, D_MODEL)),
                  _const_spec((1, D_MODEL)),
                  _const_spec((1, D_MODEL))],
        out_specs=pl.BlockSpec((TAIL_TM, D_MODEL), row_map),
        out_shape=jax.ShapeDtypeStruct((m, D_MODEL), F32),
        scratch_shapes=[pltpu.VMEM((N_FF_CHUNKS, SUBLANES, 2 * FF_CHUNK), F32),
                        pltpu.VMEM((TAIL_TM, D_MODEL), F32)],
        compiler_params=pltpu.CompilerParams(dimension_semantics=("parallel", "arbitrary"),
                                             vmem_limit_bytes=VMEM_LIMIT),
        name="layer_tail",
    )(x2d, pool_o, ssd_o, sb_o, p2d, wout, g1, b1, wup, fcw, fcb, wdown, wgate, wproj, g2, b2)


def _chunk_cols(a):
    lead = a.shape[:-1]
    g = a[..., :D_FF].reshape(*lead, N_FF_CHUNKS, FF_CHUNK)
    v = a[..., D_FF:].reshape(*lead, N_FF_CHUNKS, FF_CHUNK)
    gv = jnp.concatenate([g, v], axis=-1)
    return jnp.moveaxis(gv, -2, 0)


def kernel(x, p, w_in, pool_w, pool_scale, ssd_conv_w, ssd_conv_b, ssd_dt_bias, ssd_a_log, ssd_d,
           ssd_norm_w, w_out, ln1_g, ln1_b, ffn_w_up, ffn_conv_w, ffn_conv_b, ffn_w_down, ln2_g, ln2_b,
           ple_w_gate, ple_w_proj):
    batch, seq, _ = x.shape
    m = batch * seq
    x2d = x.reshape(m, D_MODEL)
    pad6 = lambda a: jnp.pad(a, (0, DT_PAD - SSD_HEADS)).reshape(1, DT_PAD)
    for i in range(DEPTH):
        wi = w_in[i]
        w_in_r = jnp.concatenate(
            [wi[:, :COL_DT], jnp.pad(wi[:, COL_DT:COL_Q], ((0, 0), (0, DT_PAD - SSD_HEADS))), wi[:, COL_Q:]],
            axis=1).astype(BF16)
        u, z, xbc, dt, q, k, v = _in_proj(x2d, w_in_r)

        w_bd = jax.scipy.linalg.block_diag(*[pool_w[i, g] for g in range(POOL_GROUPS)]).astype(BF16)
        pool_o = _pool(u, w_bd, pool_scale[i].reshape(1, POOL_WIDTH), batch, seq)

        ssd_o = _ssd(z, xbc, dt, ssd_conv_w[i], ssd_conv_b[i].reshape(1, -1), pad6(ssd_dt_bias[i]),
                     pad6(ssd_a_log[i]), jnp.repeat(ssd_d[i], HEAD_DIM).reshape(1, SSD_WIDTH),
                     ssd_norm_w[i].reshape(1, SSD_WIDTH), batch, seq)

        sb_o = _sb_attention(q, k, v, batch, seq)

        x2d = _tail(x2d, pool_o, ssd_o, sb_o, p[i].reshape(m, PLE_DIM),
                    w_out[i].astype(BF16), ln1_g[i].reshape(1, -1), ln1_b[i].reshape(1, -1),
                    _chunk_cols(ffn_w_up[i]).astype(BF16), _chunk_cols(ffn_conv_w[i]),
                    _chunk_cols(ffn_conv_b[i].reshape(1, -1)),
                    ffn_w_down[i].reshape(N_FF_CHUNKS, FF_CHUNK, D_MODEL).astype(BF16),
                    ple_w_gate[i].astype(BF16), ple_w_proj[i].astype(BF16),
                    ln2_g[i].reshape(1, -1), ln2_b[i].reshape(1, -1), batch, seq)
    return x2d.reshape(batch, seq, D_MODEL)
```

```python
import functools

import jax
import jax.numpy as jnp
from jax import lax
from jax.experimental import pallas as pl
from jax.experimental.pallas import tpu as pltpu

F32 = jnp.float32
BF16 = jnp.bfloat16

D_MODEL = 1024
DEPTH = 4
PLE_DIM = 256
DEEPNORM_ALPHA = (2 * DEPTH) ** 0.25
LN_EPS = 1e-5
RMS_EPS = 1e-6
HEAD_DIM = 64
POOL_WIDTH = 256
POOL_GROUPS = 4
POOL_GROUP_DIM = 64
POOL_WINDOWS = (2, 4, 8, 16)
SSD_WIDTH = 384
SSD_HEADS = 6
SSD_GROUPS = 2
SSD_GROUP_WIDTH = SSD_WIDTH // SSD_GROUPS
SSD_STATE = 128
SSD_CONV = 4
SSD_XBC_WIDTH = 896
SB_WIDTH = 384
D_FF = 2816
FFN_CONV = 3
COL_Z = 256
COL_XBC = COL_Z + SSD_WIDTH
COL_DT = COL_XBC + SSD_XBC_WIDTH
COL_Q = COL_DT + SSD_HEADS
COL_K = COL_Q + SB_WIDTH
COL_V = COL_K + SB_WIDTH

LANES = 128
SUBLANES = 8
DT_PAD = LANES
VMEM_LIMIT = 56 * 1024 * 1024

IN_TM = 1024
SSD_L = 256
SB_T = 256
SB_UNDERFLOW = 105.0
TAIL_TM = 256
FF_CHUNK = 256
N_FF_CHUNKS = D_FF // FF_CHUNK

IN_GROUPS = (("pool", POOL_WIDTH, BF16), ("z", SSD_WIDTH, BF16), ("xbc", SSD_XBC_WIDTH, BF16),
             ("dt", DT_PAD, F32), ("q", SB_WIDTH, BF16), ("k", SB_WIDTH, BF16), ("v", SB_WIDTH, BF16))
IN_COLS_PADDED = sum(w for _, w, _ in IN_GROUPS)


def _dot(a, b):
    return jnp.dot(a, b, preferred_element_type=F32)


def _dot_nt(a, b):
    return lax.dot_general(a, b, (((1,), (1,)), ((), ())), preferred_element_type=F32)


def _split3(x):
    hi = x.astype(BF16)
    r1 = x - hi.astype(F32)
    mid = r1.astype(BF16)
    lo = (r1 - mid.astype(F32)).astype(BF16)
    return hi, mid, lo


def _dot_sel_rhs(x, sel):
    hi, mid, lo = _split3(x)
    return _dot(hi, sel) + _dot(mid, sel) + _dot(lo, sel)


def _dot_sel_lhs(sel, x):
    hi, mid, lo = _split3(x)
    return _dot(sel, hi) + _dot(sel, mid) + _dot(sel, lo)


def _sigmoid(x):
    return 1.0 / (1.0 + jnp.exp(-x))


def _softplus(x):
    return jnp.maximum(x, 0.0) + jnp.log(1.0 + jnp.exp(-jnp.abs(x)))


def _layer_norm(y, g, b):
    mu = jnp.mean(y, axis=-1, keepdims=True)
    yc = y - mu
    var = jnp.mean(yc * yc, axis=-1, keepdims=True)
    return yc * lax.rsqrt(var + LN_EPS) * g + b


def _shift_rows(prev_rows, x, d):
    n = x.shape[0]
    cat = jnp.concatenate([prev_rows, x], axis=0)
    return pltpu.roll(cat, d, axis=0)[SUBLANES:SUBLANES + n]


def _const_spec(shape):
    zeros = (0,) * len(shape)
    return pl.BlockSpec(shape, lambda *_: zeros, pipeline_mode=pl.Buffered(1))


def _in_proj_kernel(x_ref, w_ref, *out_refs):
    h = _dot(x_ref[...].astype(BF16), w_ref[...])
    col = 0
    for (_, width, dtype), o_ref in zip(IN_GROUPS, out_refs):
        o_ref[...] = h[:, col:col + width].astype(dtype)
        col += width


def _in_proj(x2d, w_in_r):
    m = x2d.shape[0]
    return pl.pallas_call(
        _in_proj_kernel,
        grid=(m // IN_TM,),
        in_specs=[pl.BlockSpec((IN_TM, D_MODEL), lambda i: (i, 0)),
                  _const_spec((D_MODEL, IN_COLS_PADDED))],
        out_specs=[pl.BlockSpec((IN_TM, w), lambda i: (i, 0)) for _, w, _ in IN_GROUPS],
        out_shape=[jax.ShapeDtypeStruct((m, w), dt) for _, w, dt in IN_GROUPS],
        compiler_params=pltpu.CompilerParams(dimension_semantics=("parallel",),
                                             vmem_limit_bytes=VMEM_LIMIT),
        name="in_proj",
    )(x2d, w_in_r)


def _pool_kernel(u_ref, w_ref, scale_ref, o_ref):
    u = u_ref[...].astype(F32)
    row = lax.broadcasted_iota(jnp.int32, u.shape, 0)
    lane = lax.broadcasted_iota(jnp.int32, u.shape, 1)

    def shifted(x, d):
        return jnp.where(row >= d, pltpu.roll(x, d, axis=0), 0.0)

    s2 = u + shifted(u, 1)
    s4 = s2 + shifted(s2, 2)
    s8 = s4 + shifted(s4, 4)
    s16 = s8 + shifted(s8, 8)
    group = lane // POOL_GROUP_DIM
    win_sum = jnp.where(group == 0, s2, jnp.where(group == 1, s4, jnp.where(group == 2, s8, s16)))
    win = jnp.where(group == 0, 2, jnp.where(group == 1, 4, jnp.where(group == 2, 8, 16)))
    count = jnp.minimum(row + 1, win).astype(F32)
    pooled = win_sum / count - u
    o_ref[...] = (_dot(pooled.astype(BF16), w_ref[...]) * scale_ref[...]).astype(o_ref.dtype)


def _pool(u2d, w_bd, scale, batch, seq):
    m = u2d.shape[0]
    return pl.pallas_call(
        _pool_kernel,
        grid=(batch,),
        in_specs=[pl.BlockSpec((seq, POOL_WIDTH), lambda b: (b, 0)),
                  _const_spec((POOL_WIDTH, POOL_WIDTH)),
                  _const_spec((1, POOL_WIDTH))],
        out_specs=pl.BlockSpec((seq, POOL_WIDTH), lambda b: (b, 0)),
        out_shape=jax.ShapeDtypeStruct((m, POOL_WIDTH), BF16),
        compiler_params=pltpu.CompilerParams(dimension_semantics=("parallel",),
                                             vmem_limit_bytes=VMEM_LIMIT),
        name="pool_mixer",
    )(u2d, w_bd, scale)


def _ssd_kernel(z_ref, xbc_ref, dt_ref, cw_ref, cb_ref, dtb_ref, alog_ref, dskip_ref, nw_ref,
                o_ref, tail_ref, state_ref):
    L = SSD_L

    @pl.when(pl.program_id(1) == 0)
    def _():
        tail_ref[...] = jnp.zeros_like(tail_ref)
        state_ref[...] = jnp.zeros_like(state_ref)

    x = xbc_ref[...].astype(F32)
    prev = tail_ref[...]
    conv = cb_ref[...] + cw_ref[3:4, :] * x
    for d in (1, 2, 3):
        conv = conv + cw_ref[3 - d:4 - d, :] * _shift_rows(prev, x, d)
    tail_ref[...] = x[L - SUBLANES:, :]
    xc = conv * _sigmoid(conv)
    xs = xc[:, :SSD_WIDTH]

    dt = _softplus(dt_ref[...] + dtb_ref[...])
    adt = dt * (-jnp.exp(alog_ref[...]))
    r_ll = lax.broadcasted_iota(jnp.int32, (L, L), 0)
    c_ll = lax.broadcasted_iota(jnp.int32, (L, L), 1)
    causal = c_ll <= r_ll
    lower = jnp.where(causal, 1.0, 0.0).astype(BF16)
    a_cum = _dot_sel_lhs(lower, adt)
    a_cum_t = a_cum.T

    e_r = lax.broadcasted_iota(jnp.int32, (DT_PAD, SSD_WIDTH), 0)
    e_c = lax.broadcasted_iota(jnp.int32, (DT_PAD, SSD_WIDTH), 1)
    expand = jnp.where(e_c // HEAD_DIM == e_r, 1.0, 0.0).astype(BF16)
    dt_e = _dot_sel_rhs(dt, expand)
    a_cum_e = _dot_sel_rhs(a_cum, expand)
    a_tot_e = a_cum_e[L - 1:L, :]

    xdt = xs * dt_e
    xdt_b = xdt.astype(BF16)
    xw_b = (xdt * jnp.exp(a_tot_e - a_cum_e)).astype(BF16)
    lane_w = lax.broadcasted_iota(jnp.int32, (L, SSD_WIDTH), 1)
    state = state_ref[...]
    state_b = state.astype(BF16)

    heads = []
    y_off = None
    new_state = None
    for g in range(SSD_GROUPS):
        b_off = SSD_WIDTH + g * SSD_STATE
        c_off = SSD_WIDTH + SSD_GROUPS * SSD_STATE + g * SSD_STATE
        bm = xc[:, b_off:b_off + SSD_STATE]
        cm_b = xc[:, c_off:c_off + SSD_STATE].astype(BF16)
        cb = _dot_nt(cm_b, bm.astype(BF16))
        for r in range(SSD_HEADS // SSD_GROUPS):
            h = g * (SSD_HEADS // SSD_GROUPS) + r
            seg = a_cum[:, h:h + 1] - a_cum_t[h:h + 1, :]
            decay = jnp.exp(jnp.where(causal, seg, -1e30))
            wmat = (cb * decay).astype(BF16)
            pair = (h // 2) * LANES
            heads.append(_dot(wmat, xdt_b[:, pair:pair + LANES]))
        st_g = _dot(bm.T.astype(BF16), xw_b)
        yo_g = _dot(cm_b, state_b)
        if g == 0:
            new_state, y_off = st_g, yo_g
        else:
            in_g0 = lax.broadcasted_iota(jnp.int32, st_g.shape, 1) < SSD_GROUP_WIDTH
            new_state = jnp.where(in_g0, new_state, st_g)
            y_off = jnp.where(lane_w < SSD_GROUP_WIDTH, y_off, yo_g)

    state_ref[...] = state * jnp.exp(a_tot_e) + new_state

    lane_p = lax.broadcasted_iota(jnp.int32, (L, LANES), 1)
    y_diag = jnp.concatenate(
        [jnp.where(lane_p < HEAD_DIM, heads[2 * p], heads[2 * p + 1]) for p in range(SSD_HEADS // 2)],
        axis=1)
    y = y_diag + y_off * jnp.exp(a_cum_e) + xs * dskip_ref[...]

    zf = z_ref[...].astype(F32)
    hg = y * (zf * _sigmoid(zf))
    sq = hg * hg
    in_g0 = lane_w < SSD_GROUP_WIDTH
    ms0 = jnp.sum(jnp.where(in_g0, sq, 0.0), axis=-1, keepdims=True) * (1.0 / SSD_GROUP_WIDTH)
    ms1 = jnp.sum(jnp.where(in_g0, 0.0, sq), axis=-1, keepdims=True) * (1.0 / SSD_GROUP_WIDTH)
    inv = jnp.where(in_g0, lax.rsqrt(ms0 + RMS_EPS), lax.rsqrt(ms1 + RMS_EPS))
    o_ref[...] = (hg * inv * nw_ref[...]).astype(o_ref.dtype)


def _ssd(z2d, xbc2d, dt2d, cw, cb, dtb, alog, dskip, nw, batch, seq):
    m = z2d.shape[0]
    nc = seq // SSD_L
    row_map = lambda b, c: (b * nc + c, 0)
    return pl.pallas_call(
        _ssd_kernel,
        grid=(batch, nc),
        in_specs=[pl.BlockSpec((SSD_L, SSD_WIDTH), row_map),
                  pl.BlockSpec((SSD_L, SSD_XBC_WIDTH), row_map),
                  pl.BlockSpec((SSD_L, DT_PAD), row_map),
                  _const_spec((SSD_CONV, SSD_XBC_WIDTH)),
                  _const_spec((1, SSD_XBC_WIDTH)),
                  _const_spec((1, DT_PAD)),
                  _const_spec((1, DT_PAD)),
                  _const_spec((1, SSD_WIDTH)),
                  _const_spec((1, SSD_WIDTH))],
        out_specs=pl.BlockSpec((SSD_L, SSD_WIDTH), row_map),
        out_shape=jax.ShapeDtypeStruct((m, SSD_WIDTH), BF16),
        scratch_shapes=[pltpu.VMEM((SUBLANES, SSD_XBC_WIDTH), F32),
                        pltpu.VMEM((SSD_STATE, SSD_WIDTH), F32)],
        compiler_params=pltpu.CompilerParams(dimension_semantics=("parallel", "arbitrary"),
                                             vmem_limit_bytes=VMEM_LIMIT),
        name="ssd_mixer",
    )(z2d, xbc2d, dt2d, cw, cb, dtb, alog, dskip, nw)


def _sb_kernel(q_ref, k_ref, v_ref, o_ref):
    T = SB_T
    nq = q_ref.shape[0] // T
    lane = lax.broadcasted_iota(jnp.int32, (T, LANES), 1)
    r_tt = lax.broadcasted_iota(jnp.int32, (T, T), 0)
    c_tt = lax.broadcasted_iota(jnp.int32, (T, T), 1)
    strict = c_tt < r_tt
    later = jnp.where(r_tt > c_tt, 1.0, 0.0).astype(BF16)
    head_lanes = [(lane >= HEAD_DIM) == bool(hh) for hh in range(2)]

    def rows(ref, j):
        return ref[pl.ds(pl.multiple_of(j * T, T), T), :]

    def head_queries(qi):
        qs = rows(q_ref, qi) * (HEAD_DIM ** -0.5)
        return [jnp.where(m, qs, jnp.zeros_like(qs)) for m in head_lanes]

    def tile_terms(z, diagonal):
        sp = _softplus(z)
        spm = jnp.where(strict, sp, 0.0) if diagonal else sp
        right = _dot(spm.astype(BF16), later)
        return z - sp, right, right[:, 0:1] + spm[:, 0:1]

    def tile_out(logsig, right, carry, vj, diagonal):
        w = jnp.exp(logsig - right if carry is None else logsig - right - carry)
        if diagonal:
            w = jnp.where(strict, w, 0.0)
        return _dot(w.astype(BF16), vj)

    def store(qi, accs):
        o_ref[pl.ds(pl.multiple_of(qi * T, T), T), :] = jnp.where(
            lane < HEAD_DIM, accs[0], accs[1]).astype(o_ref.dtype)

    zs = [_dot_nt(qh, rows(k_ref, 0)) for qh in head_queries(0)]
    terms = [tile_terms(z, True) for z in zs]
    store(0, [tile_out(ls, right, None, rows(v_ref, 0), True) for ls, right, _ in terms])

    def q_tile(qi, _):
        qhs = head_queries(qi)
        kd, vd = rows(k_ref, qi), rows(v_ref, qi)
        ko, vo = rows(k_ref, qi - 1), rows(v_ref, qi - 1)
        zs = [(_dot_nt(qh, kd), _dot_nt(qh, ko)) for qh in qhs]
        terms = [(tile_terms(zd, True), tile_terms(zo, False)) for zd, zo in zs]
        accs, carries = [], []
        for (ls_d, right_d, tot_d), (ls_o, right_o, tot_o) in terms:
            accs.append(tile_out(ls_d, right_d, None, vd, True) + tile_out(ls_o, right_o, tot_d, vo, False))
            carries.append(tot_d + tot_o)

        def cond(st):
            return jnp.logical_and(st[0] >= 0, st[1] < SB_UNDERFLOW)

        def body(st):
            j, _, c0, a0, c1, a1 = st
            kj, vj = rows(k_ref, j), rows(v_ref, j)
            terms = [tile_terms(z, False) for z in [_dot_nt(qh, kj) for qh in qhs]]
            new = [(c + tot, a + tile_out(ls, right, c, vj, False))
                   for (ls, right, tot), c, a in zip(terms, (c0, c1), (a0, a1))]
            cmin = jnp.min(jnp.minimum(new[0][0], new[1][0]))
            return j - 1, cmin, new[0][0], new[0][1], new[1][0], new[1][1]

        cmin0 = jnp.min(jnp.minimum(carries[0], carries[1]))
        st = lax.while_loop(cond, body, (qi - 2, cmin0, carries[0], accs[0], carries[1], accs[1]))
        store(qi, (st[3], st[5]))
        return 0

    lax.fori_loop(1, nq, q_tile, 0)


def _sb_attention(q2d, k2d, v2d, batch, seq):
    m = q2d.shape[0]
    pairs = SB_WIDTH // LANES
    blk = pl.BlockSpec((seq, LANES), lambda b, p: (b, p))
    return pl.pallas_call(
        _sb_kernel,
        grid=(batch, pairs),
        in_specs=[blk, blk, blk],
        out_specs=blk,
        out_shape=jax.ShapeDtypeStruct((m, SB_WIDTH), BF16),
        compiler_params=pltpu.CompilerParams(dimension_semantics=("parallel", "parallel"),
                                             vmem_limit_bytes=VMEM_LIMIT),
        name="sb_attention",
    )(q2d, k2d, v2d)


def _tail_kernel(x_ref, pool_ref, ssd_ref, sb_ref, p_ref, wout_ref, g1_ref, b1_ref,
                 wup_ref, fcw_ref, fcb_ref, wdown_ref, wgate_ref, wproj_ref, g2_ref, b2_ref,
                 o_ref, carry_ref, acc_ref):
    tm = TAIL_TM

    @pl.when(pl.program_id(1) == 0)
    def _():
        carry_ref[...] = jnp.zeros_like(carry_ref)

    mix = _dot(jnp.concatenate([pool_ref[...], ssd_ref[...], sb_ref[...]], axis=1), wout_ref[...])
    x1 = _layer_norm(DEEPNORM_ALPHA * x_ref[...] + mix, g1_ref[...], b1_ref[...])
    xb = x1.astype(BF16)

    gate = _sigmoid(_dot(xb, wgate_ref[...]))
    acc_ref[...] = DEEPNORM_ALPHA * x1 + gate * _dot(p_ref[...].astype(BF16), wproj_ref[...])

    def chunk_cols(ref, c):
        g0, v0 = c * FF_CHUNK, D_FF + c * FF_CHUNK
        return jnp.concatenate([ref[:, g0:g0 + FF_CHUNK], ref[:, v0:v0 + FF_CHUNK]], axis=1)

    next_up = _dot(xb, chunk_cols(wup_ref, 0))
    for c in range(N_FF_CHUNKS):
        up = next_up
        if c + 1 < N_FF_CHUNKS:
            next_up = _dot(xb, chunk_cols(wup_ref, c + 1))
        prev = carry_ref[c]
        cw = chunk_cols(fcw_ref, c)
        conv = (chunk_cols(fcb_ref, c) + cw[2:3, :] * up
                + cw[1:2, :] * _shift_rows(prev, up, 1)
                + cw[0:1, :] * _shift_rows(prev, up, 2))
        carry_ref[c] = up[tm - SUBLANES:, :]
        gl = conv[:, :FF_CHUNK]
        act = (gl * _sigmoid(gl) * conv[:, FF_CHUNK:]).astype(BF16)
        acc_ref[...] += _dot(act, wdown_ref[c * FF_CHUNK:(c + 1) * FF_CHUNK, :])

    o_ref[...] = _layer_norm(acc_ref[...], g2_ref[...], b2_ref[...])


def _tail(x2d, pool_o, ssd_o, sb_o, p2d, wout, g1, b1, wup, fcw, fcb, wdown, wgate, wproj, g2, b2,
          batch, seq):
    m = x2d.shape[0]
    nt = seq // TAIL_TM
    row_map = lambda b, i: (b * nt + i, 0)
    return pl.pallas_call(
        _tail_kernel,
        grid=(batch, nt),
        in_specs=[pl.BlockSpec((TAIL_TM, D_MODEL), row_map),
                  pl.BlockSpec((TAIL_TM, POOL_WIDTH), row_map),
                  pl.BlockSpec((TAIL_TM, SSD_WIDTH), row_map),
                  pl.BlockSpec((TAIL_TM, SB_WIDTH), row_map),
                  pl.BlockSpec((TAIL_TM, PLE_DIM), row_map),
                  _const_spec((D_MODEL, D_MODEL)),
                  _const_spec((1, D_MODEL)),
                  _const_spec((1, D_MODEL)),
                  _const_spec((D_MODEL, 2 * D_FF)),
                  _const_spec((FFN_CONV, 2 * D_FF)),
                  _const_spec((1, 2 * D_FF)),
                  _const_spec((D_FF, D_MODEL)),
                  _const_spec((D_MODEL, D_MODEL)),
                  _const_spec((PLE_DIM, D_MODEL)),
                  _const_spec((1, D_MODEL)),
                  _const_spec((1, D_MODEL))],
        out_specs=pl.BlockSpec((TAIL_TM, D_MODEL), row_map),
        out_shape=jax.ShapeDtypeStruct((m, D_MODEL), F32),
        scratch_shapes=[pltpu.VMEM((N_FF_CHUNKS, SUBLANES, 2 * FF_CHUNK), F32),
                        pltpu.VMEM((TAIL_TM, D_MODEL), F32)],
        compiler_params=pltpu.CompilerParams(dimension_semantics=("parallel", "arbitrary"),
                                             vmem_limit_bytes=VMEM_LIMIT),
        name="layer_tail",
    )(x2d, pool_o, ssd_o, sb_o, p2d, wout, g1, b1, wup, fcw, fcb, wdown, wgate, wproj, g2, b2)


def kernel(x, p, w_in, pool_w, pool_scale, ssd_conv_w, ssd_conv_b, ssd_dt_bias, ssd_a_log, ssd_d,
           ssd_norm_w, w_out, ln1_g, ln1_b, ffn_w_up, ffn_conv_w, ffn_conv_b, ffn_w_down, ln2_g, ln2_b,
           ple_w_gate, ple_w_proj):
    batch, seq, _ = x.shape
    m = batch * seq
    x2d = x.reshape(m, D_MODEL)
    pad6 = lambda a: jnp.pad(a, (0, DT_PAD - SSD_HEADS)).reshape(1, DT_PAD)
    for i in range(DEPTH):
        wi = w_in[i]
        w_in_r = jnp.concatenate(
            [wi[:, :COL_DT], jnp.pad(wi[:, COL_DT:COL_Q], ((0, 0), (0, DT_PAD - SSD_HEADS))), wi[:, COL_Q:]],
            axis=1).astype(BF16)
        u, z, xbc, dt, q, k, v = _in_proj(x2d, w_in_r)

        w_bd = jax.scipy.linalg.block_diag(*[pool_w[i, g] for g in range(POOL_GROUPS)]).astype(BF16)
        pool_o = _pool(u, w_bd, pool_scale[i].reshape(1, POOL_WIDTH), batch, seq)

        ssd_o = _ssd(z, xbc, dt, ssd_conv_w[i], ssd_conv_b[i].reshape(1, -1), pad6(ssd_dt_bias[i]),
                     pad6(ssd_a_log[i]), jnp.repeat(ssd_d[i], HEAD_DIM).reshape(1, SSD_WIDTH),
                     ssd_norm_w[i].reshape(1, SSD_WIDTH), batch, seq)

        sb_o = _sb_attention(q, k, v, batch, seq)

        x2d = _tail(x2d, pool_o, ssd_o, sb_o, p[i].reshape(m, PLE_DIM),
                    w_out[i].astype(BF16), ln1_g[i].reshape(1, -1), ln1_b[i].reshape(1, -1),
                    ffn_w_up[i].astype(BF16), ffn_conv_w[i], ffn_conv_b[i].reshape(1, -1),
                    ffn_w_down[i].astype(BF16),
                    ple_w_gate[i].astype(BF16), ple_w_proj[i].astype(BF16),
                    ln2_g[i].reshape(1, -1), ln2_b[i].reshape(1, -1), batch, seq)
    return x2d.reshape(batch, seq, D_MODEL)
```

```python
import functools

import jax
import jax.numpy as jnp
from jax import lax
from jax.experimental import pallas as pl
from jax.experimental.pallas import tpu as pltpu

F32 = jnp.float32
BF16 = jnp.bfloat16

D_MODEL = 1024
DEPTH = 4
PLE_DIM = 256
DEEPNORM_ALPHA = (2 * DEPTH) ** 0.25
LN_EPS = 1e-5
RMS_EPS = 1e-6
HEAD_DIM = 64
POOL_WIDTH = 256
POOL_GROUPS = 4
POOL_GROUP_DIM = 64
POOL_WINDOWS = (2, 4, 8, 16)
SSD_WIDTH = 384
SSD_HEADS = 6
SSD_GROUPS = 2
SSD_GROUP_WIDTH = SSD_WIDTH // SSD_GROUPS
SSD_STATE = 128
SSD_CONV = 4
SSD_XBC_WIDTH = 896
SB_WIDTH = 384
D_FF = 2816
FFN_CONV = 3
COL_Z = 256
COL_XBC = COL_Z + SSD_WIDTH
COL_DT = COL_XBC + SSD_XBC_WIDTH
COL_Q = COL_DT + SSD_HEADS
COL_K = COL_Q + SB_WIDTH
COL_V = COL_K + SB_WIDTH

LANES = 128
SUBLANES = 8
DT_PAD = LANES
VMEM_LIMIT = 56 * 1024 * 1024

IN_TM = 1024
SSD_L = 256
SB_T = 256
SB_UNDERFLOW = 105.0
TAIL_TM = 256
FF_CHUNK = 256
N_FF_CHUNKS = D_FF // FF_CHUNK

IN_GROUPS = (("pool", POOL_WIDTH, BF16), ("z", SSD_WIDTH, BF16), ("xbc", SSD_XBC_WIDTH, BF16),
             ("dt", DT_PAD, F32), ("q", SB_WIDTH, BF16), ("k", SB_WIDTH, BF16), ("v", SB_WIDTH, BF16))
IN_COLS_PADDED = sum(w for _, w, _ in IN_GROUPS)


def _dot(a, b):
    return jnp.dot(a, b, preferred_element_type=F32)


def _dot_nt(a, b):
    return lax.dot_general(a, b, (((1,), (1,)), ((), ())), preferred_element_type=F32)


def _split3(x):
    hi = x.astype(BF16)
    r1 = x - hi.astype(F32)
    mid = r1.astype(BF16)
    lo = (r1 - mid.astype(F32)).astype(BF16)
    return hi, mid, lo


def _dot_sel_rhs(x, sel):
    hi, mid, lo = _split3(x)
    return _dot(hi, sel) + _dot(mid, sel) + _dot(lo, sel)


def _dot_sel_lhs(sel, x):
    hi, mid, lo = _split3(x)
    return _dot(sel, hi) + _dot(sel, mid) + _dot(sel, lo)


def _sigmoid(x):
    return 1.0 / (1.0 + jnp.exp(-x))


def _softplus(x):
    return jnp.maximum(x, 0.0) + jnp.log(1.0 + jnp.exp(-jnp.abs(x)))


def _layer_norm(y, g, b):
    mu = jnp.mean(y, axis=-1, keepdims=True)
    yc = y - mu
    var = jnp.mean(yc * yc, axis=-1, keepdims=True)
    return yc * lax.rsqrt(var + LN_EPS) * g + b


def _shift_rows(prev_rows, x, d):
    n = x.shape[0]
    cat = jnp.concatenate([prev_rows, x], axis=0)
    return pltpu.roll(cat, d, axis=0)[SUBLANES:SUBLANES + n]


def _const_spec(shape):
    zeros = (0,) * len(shape)
    return pl.BlockSpec(shape, lambda *_: zeros, pipeline_mode=pl.Buffered(1))


def _layer_spec(shape, layer):
    index = (layer,) + (0,) * len(shape)
    return pl.BlockSpec((None,) + tuple(shape), lambda *_: index, pipeline_mode=pl.Buffered(1))


def _in_proj_kernel(x_ref, w_ref, *out_refs):
    h = _dot(x_ref[...].astype(BF16), w_ref[...])
    col = 0
    for (_, width, dtype), o_ref in zip(IN_GROUPS, out_refs):
        o_ref[...] = h[:, col:col + width].astype(dtype)
        col += width


def _in_proj(x2d, w_in_r, layer):
    m = x2d.shape[0]
    return pl.pallas_call(
        _in_proj_kernel,
        grid=(m // IN_TM,),
        in_specs=[pl.BlockSpec((IN_TM, D_MODEL), lambda i: (i, 0)),
                  _layer_spec((D_MODEL, IN_COLS_PADDED), layer)],
        out_specs=[pl.BlockSpec((IN_TM, w), lambda i: (i, 0)) for _, w, _ in IN_GROUPS],
        out_shape=[jax.ShapeDtypeStruct((m, w), dt) for _, w, dt in IN_GROUPS],
        compiler_params=pltpu.CompilerParams(dimension_semantics=("parallel",),
                                             vmem_limit_bytes=VMEM_LIMIT),
        name="in_proj",
    )(x2d, w_in_r)


def _pool_kernel(u_ref, w_ref, scale_ref, o_ref):
    u = u_ref[...].astype(F32)
    row = lax.broadcasted_iota(jnp.int32, u.shape, 0)
    lane = lax.broadcasted_iota(jnp.int32, u.shape, 1)

    def shifted(x, d):
        return jnp.where(row >= d, pltpu.roll(x, d, axis=0), 0.0)

    s2 = u + shifted(u, 1)
    s4 = s2 + shifted(s2, 2)
    s8 = s4 + shifted(s4, 4)
    s16 = s8 + shifted(s8, 8)
    group = lane // POOL_GROUP_DIM
    win_sum = jnp.where(group == 0, s2, jnp.where(group == 1, s4, jnp.where(group == 2, s8, s16)))
    win = jnp.where(group == 0, 2, jnp.where(group == 1, 4, jnp.where(group == 2, 8, 16)))
    count = jnp.minimum(row + 1, win).astype(F32)
    pooled = win_sum / count - u
    o_ref[...] = (_dot(pooled.astype(BF16), w_ref[...]) * scale_ref[...]).astype(o_ref.dtype)


def _pool(u2d, w_bd, scale, batch, seq):
    m = u2d.shape[0]
    return pl.pallas_call(
        _pool_kernel,
        grid=(batch,),
        in_specs=[pl.BlockSpec((seq, POOL_WIDTH), lambda b: (b, 0)),
                  _const_spec((POOL_WIDTH, POOL_WIDTH)),
                  _const_spec((1, POOL_WIDTH))],
        out_specs=pl.BlockSpec((seq, POOL_WIDTH), lambda b: (b, 0)),
        out_shape=jax.ShapeDtypeStruct((m, POOL_WIDTH), BF16),
        compiler_params=pltpu.CompilerParams(dimension_semantics=("parallel",),
                                             vmem_limit_bytes=VMEM_LIMIT),
        name="pool_mixer",
    )(u2d, w_bd, scale)


def _ssd_kernel(z_ref, xbc_ref, dt_ref, cw_ref, cb_ref, dtb_ref, alog_ref, dskip_ref, nw_ref,
                o_ref, tail_ref, state_ref):
    L = SSD_L

    @pl.when(pl.program_id(1) == 0)
    def _():
        tail_ref[...] = jnp.zeros_like(tail_ref)
        state_ref[...] = jnp.zeros_like(state_ref)

    x = xbc_ref[...].astype(F32)
    prev = tail_ref[...]
    conv = cb_ref[...] + cw_ref[3:4, :] * x
    for d in (1, 2, 3):
        conv = conv + cw_ref[3 - d:4 - d, :] * _shift_rows(prev, x, d)
    tail_ref[...] = x[L - SUBLANES:, :]
    xc = conv * _sigmoid(conv)
    xs = xc[:, :SSD_WIDTH]

    dt = _softplus(dt_ref[...] + dtb_ref[...])
    adt = dt * (-jnp.exp(alog_ref[...]))
    r_ll = lax.broadcasted_iota(jnp.int32, (L, L), 0)
    c_ll = lax.broadcasted_iota(jnp.int32, (L, L), 1)
    causal = c_ll <= r_ll
    lower = jnp.where(causal, 1.0, 0.0).astype(BF16)
    a_cum = _dot_sel_lhs(lower, adt)
    a_cum_t = a_cum.T

    e_r = lax.broadcasted_iota(jnp.int32, (DT_PAD, SSD_WIDTH), 0)
    e_c = lax.broadcasted_iota(jnp.int32, (DT_PAD, SSD_WIDTH), 1)
    expand = jnp.where(e_c // HEAD_DIM == e_r, 1.0, 0.0).astype(BF16)
    dt_e = _dot_sel_rhs(dt, expand)
    a_cum_e = _dot_sel_rhs(a_cum, expand)
    a_tot_e = a_cum_e[L - 1:L, :]

    xdt = xs * dt_e
    xdt_b = xdt.astype(BF16)
    xw_b = (xdt * jnp.exp(a_tot_e - a_cum_e)).astype(BF16)
    lane_w = lax.broadcasted_iota(jnp.int32, (L, SSD_WIDTH), 1)
    state = state_ref[...]
    state_b = state.astype(BF16)

    heads = []
    y_off = None
    new_state = None
    for g in range(SSD_GROUPS):
        b_off = SSD_WIDTH + g * SSD_STATE
        c_off = SSD_WIDTH + SSD_GROUPS * SSD_STATE + g * SSD_STATE
        bm = xc[:, b_off:b_off + SSD_STATE]
        cm_b = xc[:, c_off:c_off + SSD_STATE].astype(BF16)
        cb = _dot_nt(cm_b, bm.astype(BF16))
        for r in range(SSD_HEADS // SSD_GROUPS):
            h = g * (SSD_HEADS // SSD_GROUPS) + r
            seg = a_cum[:, h:h + 1] - a_cum_t[h:h + 1, :]
            decay = jnp.exp(jnp.where(causal, seg, -1e30))
            wmat = (cb * decay).astype(BF16)
            pair = (h // 2) * LANES
            heads.append(_dot(wmat, xdt_b[:, pair:pair + LANES]))
        st_g = _dot(bm.T.astype(BF16), xw_b)
        yo_g = _dot(cm_b, state_b)
        if g == 0:
            new_state, y_off = st_g, yo_g
        else:
            in_g0 = lax.broadcasted_iota(jnp.int32, st_g.shape, 1) < SSD_GROUP_WIDTH
            new_state = jnp.where(in_g0, new_state, st_g)
            y_off = jnp.where(lane_w < SSD_GROUP_WIDTH, y_off, yo_g)

    state_ref[...] = state * jnp.exp(a_tot_e) + new_state

    lane_p = lax.broadcasted_iota(jnp.int32, (L, LANES), 1)
    y_diag = jnp.concatenate(
        [jnp.where(lane_p < HEAD_DIM, heads[2 * p], heads[2 * p + 1]) for p in range(SSD_HEADS // 2)],
        axis=1)
    y = y_diag + y_off * jnp.exp(a_cum_e) + xs * dskip_ref[...]

    zf = z_ref[...].astype(F32)
    hg = y * (zf * _sigmoid(zf))
    sq = hg * hg
    in_g0 = lane_w < SSD_GROUP_WIDTH
    ms0 = jnp.sum(jnp.where(in_g0, sq, 0.0), axis=-1, keepdims=True) * (1.0 / SSD_GROUP_WIDTH)
    ms1 = jnp.sum(jnp.where(in_g0, 0.0, sq), axis=-1, keepdims=True) * (1.0 / SSD_GROUP_WIDTH)
    inv = jnp.where(in_g0, lax.rsqrt(ms0 + RMS_EPS), lax.rsqrt(ms1 + RMS_EPS))
    o_ref[...] = (hg * inv * nw_ref[...]).astype(o_ref.dtype)


def _ssd(z2d, xbc2d, dt2d, cw, cb, dtb, alog, dskip, nw, batch, seq):
    m = z2d.shape[0]
    nc = seq // SSD_L
    row_map = lambda b, c: (b * nc + c, 0)
    return pl.pallas_call(
        _ssd_kernel,
        grid=(batch, nc),
        in_specs=[pl.BlockSpec((SSD_L, SSD_WIDTH), row_map),
                  pl.BlockSpec((SSD_L, SSD_XBC_WIDTH), row_map),
                  pl.BlockSpec((SSD_L, DT_PAD), row_map),
                  _const_spec((SSD_CONV, SSD_XBC_WIDTH)),
                  _const_spec((1, SSD_XBC_WIDTH)),
                  _const_spec((1, DT_PAD)),
                  _const_spec((1, DT_PAD)),
                  _const_spec((1, SSD_WIDTH)),
                  _const_spec((1, SSD_WIDTH))],
        out_specs=pl.BlockSpec((SSD_L, SSD_WIDTH), row_map),
        out_shape=jax.ShapeDtypeStruct((m, SSD_WIDTH), BF16),
        scratch_shapes=[pltpu.VMEM((SUBLANES, SSD_XBC_WIDTH), F32),
                        pltpu.VMEM((SSD_STATE, SSD_WIDTH), F32)],
        compiler_params=pltpu.CompilerParams(dimension_semantics=("parallel", "arbitrary"),
                                             vmem_limit_bytes=VMEM_LIMIT),
        name="ssd_mixer",
    )(z2d, xbc2d, dt2d, cw, cb, dtb, alog, dskip, nw)


def _sb_kernel(q_ref, k_ref, v_ref, o_ref):
    T = SB_T
    nq = q_ref.shape[0] // T
    lane = lax.broadcasted_iota(jnp.int32, (T, LANES), 1)
    r_tt = lax.broadcasted_iota(jnp.int32, (T, T), 0)
    c_tt = lax.broadcasted_iota(jnp.int32, (T, T), 1)
    strict = c_tt < r_tt
    later = jnp.where(r_tt > c_tt, 1.0, 0.0).astype(BF16)
    head_lanes = [(lane >= HEAD_DIM) == bool(hh) for hh in range(2)]

    def rows(ref, j):
        return ref[pl.ds(pl.multiple_of(j * T, T), T), :]

    def head_queries(qi):
        qs = rows(q_ref, qi) * (HEAD_DIM ** -0.5)
        return [jnp.where(m, qs, jnp.zeros_like(qs)) for m in head_lanes]

    def tile_terms(z, diagonal):
        sp = _softplus(z)
        spm = jnp.where(strict, sp, 0.0) if diagonal else sp
        right = _dot(spm.astype(BF16), later)
        return z - sp, right, right[:, 0:1] + spm[:, 0:1]

    def tile_out(logsig, right, carry, vj, diagonal):
        w = jnp.exp(logsig - right if carry is None else logsig - right - carry)
        if diagonal:
            w = jnp.where(strict, w, 0.0)
        return _dot(w.astype(BF16), vj)

    def store(qi, accs):
        o_ref[pl.ds(pl.multiple_of(qi * T, T), T), :] = jnp.where(
            lane < HEAD_DIM, accs[0], accs[1]).astype(o_ref.dtype)

    def sweep_rest(qi, qhs, carries, accs):
        def cond(st):
            return jnp.logical_and(st[0] >= 0, st[1] < SB_UNDERFLOW)

        def body(st):
            j, _, c0, a0, c1, a1 = st
            kj, vj = rows(k_ref, j), rows(v_ref, j)
            terms = [tile_terms(z, False) for z in [_dot_nt(qh, kj) for qh in qhs]]
            new = [(c + tot, a + tile_out(ls, right, c, vj, False))
                   for (ls, right, tot), c, a in zip(terms, (c0, c1), (a0, a1))]
            cmin = jnp.min(jnp.minimum(new[0][0], new[1][0]))
            return j - 1, cmin, new[0][0], new[0][1], new[1][0], new[1][1]

        cmin0 = jnp.min(jnp.minimum(carries[0], carries[1]))
        st = lax.while_loop(cond, body, (qi - 2, cmin0, carries[0], accs[0], carries[1], accs[1]))
        return st[3], st[5]

    def q_tiles(tiles):
        chains = []
        qhs = [head_queries(qi) for qi, _ in tiles]
        for t, (qi, has_left) in enumerate(tiles):
            for hh in range(2):
                chains.append((t, hh, True, qi))
                if has_left:
                    chains.append((t, hh, False, qi - 1))
        zs = [_dot_nt(qhs[t][hh], rows(k_ref, kt)) for t, hh, _, kt in chains]
        terms = [tile_terms(z, diag) for z, (_, _, diag, _) in zip(zs, chains)]
        accs = [[None, None] for _ in tiles]
        carries = [[None, None] for _ in tiles]
        for (t, hh, diag, kt), (ls, right, tot) in zip(chains, terms):
            out = tile_out(ls, right, None if diag else carries[t][hh], rows(v_ref, kt), diag)
            accs[t][hh] = out if diag else accs[t][hh] + out
            carries[t][hh] = tot if diag else carries[t][hh] + tot
        for t, (qi, has_left) in enumerate(tiles):
            store(qi, sweep_rest(qi, qhs[t], carries[t], accs[t]) if has_left else accs[t])

    q_tiles([(0, False), (1, True)])

    def pair(i, _):
        q_tiles([(2 * i, True), (2 * i + 1, True)])
        return 0

    lax.fori_loop(1, nq // 2, pair, 0)


def _sb_attention(q2d, k2d, v2d, batch, seq):
    m = q2d.shape[0]
    pairs = SB_WIDTH // LANES
    blk = pl.BlockSpec((seq, LANES), lambda b, p: (b, p))
    return pl.pallas_call(
        _sb_kernel,
        grid=(batch, pairs),
        in_specs=[blk, blk, blk],
        out_specs=blk,
        out_shape=jax.ShapeDtypeStruct((m, SB_WIDTH), BF16),
        compiler_params=pltpu.CompilerParams(dimension_semantics=("parallel", "parallel"),
                                             vmem_limit_bytes=VMEM_LIMIT),
        name="sb_attention",
    )(q2d, k2d, v2d)


def _tail_kernel(x_ref, pool_ref, ssd_ref, sb_ref, p_ref, wout_ref, g1_ref, b1_ref,
                 wup_ref, fcw_ref, fcb_ref, wdown_ref, wgate_ref, wproj_ref, g2_ref, b2_ref,
                 o_ref, carry_ref, acc_ref, *, tiles_per_seq):
    tm = TAIL_TM
    j = pl.program_id(0)

    @pl.when(j == 0)
    def _():
        acc_ref[...] = jnp.zeros_like(acc_ref)

    @pl.when(j % tiles_per_seq == 0)
    def _():
        carry_ref[...] = jnp.zeros_like(carry_ref)

    o_ref[...] = _layer_norm(acc_ref[...], g2_ref[...], b2_ref[...])

    mix = _dot(jnp.concatenate([pool_ref[...], ssd_ref[...], sb_ref[...]], axis=1), wout_ref[...])
    ple = _dot(p_ref[...].astype(BF16), wproj_ref[...])
    x1 = _layer_norm(DEEPNORM_ALPHA * x_ref[...] + mix, g1_ref[...], b1_ref[...])
    xb = x1.astype(BF16)

    gate = _sigmoid(_dot(xb, wgate_ref[...]))
    acc_ref[...] = DEEPNORM_ALPHA * x1 + gate * ple

    def chunk_cols(ref, c):
        g0, v0 = c * FF_CHUNK, D_FF + c * FF_CHUNK
        return jnp.concatenate([ref[:, g0:g0 + FF_CHUNK], ref[:, v0:v0 + FF_CHUNK]], axis=1)

    next_up = _dot(xb, chunk_cols(wup_ref, 0))
    for c in range(N_FF_CHUNKS):
        up = next_up
        if c + 1 < N_FF_CHUNKS:
            next_up = _dot(xb, chunk_cols(wup_ref, c + 1))
        prev = carry_ref[c]
        cw = chunk_cols(fcw_ref, c)
        conv = (chunk_cols(fcb_ref, c) + cw[2:3, :] * up
                + cw[1:2, :] * _shift_rows(prev, up, 1)
                + cw[0:1, :] * _shift_rows(prev, up, 2))
        carry_ref[c] = up[tm - SUBLANES:, :]
        gl = conv[:, :FF_CHUNK]
        act = (gl * _sigmoid(gl) * conv[:, FF_CHUNK:]).astype(BF16)
        acc_ref[...] += _dot(act, wdown_ref[c * FF_CHUNK:(c + 1) * FF_CHUNK, :])


def _tail(x2d, pool_o, ssd_o, sb_o, p2d, wout, g1, b1, wup, fcw, fcb, wdown, wgate, wproj, g2, b2, seq,
          layer):
    m = x2d.shape[0]
    nt = m // TAIL_TM
    cur_map = lambda j: (jnp.minimum(j, nt - 1), 0)
    p_map = lambda j: (layer * nt + jnp.minimum(j, nt - 1), 0)
    prev_map = lambda j: (jnp.maximum(j - 1, 0), 0)
    return pl.pallas_call(
        functools.partial(_tail_kernel, tiles_per_seq=seq // TAIL_TM),
        grid=(nt + 1,),
        in_specs=[pl.BlockSpec((TAIL_TM, D_MODEL), cur_map),
                  pl.BlockSpec((TAIL_TM, POOL_WIDTH), cur_map),
                  pl.BlockSpec((TAIL_TM, SSD_WIDTH), cur_map),
                  pl.BlockSpec((TAIL_TM, SB_WIDTH), cur_map),
                  pl.BlockSpec((TAIL_TM, PLE_DIM), p_map),
                  _layer_spec((D_MODEL, D_MODEL), layer),
                  _const_spec((1, D_MODEL)),
                  _const_spec((1, D_MODEL)),
                  _layer_spec((D_MODEL, 2 * D_FF), layer),
                  _const_spec((FFN_CONV, 2 * D_FF)),
                  _const_spec((1, 2 * D_FF)),
                  _layer_spec((D_FF, D_MODEL), layer),
                  _layer_spec((D_MODEL, D_MODEL), layer),
                  _layer_spec((PLE_DIM, D_MODEL), layer),
                  _const_spec((1, D_MODEL)),
                  _const_spec((1, D_MODEL))],
        out_specs=pl.BlockSpec((TAIL_TM, D_MODEL), prev_map),
        out_shape=jax.ShapeDtypeStruct((m, D_MODEL), F32),
        scratch_shapes=[pltpu.VMEM((N_FF_CHUNKS, SUBLANES, 2 * FF_CHUNK), F32),
                        pltpu.VMEM((TAIL_TM, D_MODEL), F32)],
        compiler_params=pltpu.CompilerParams(dimension_semantics=("arbitrary",),
                                             vmem_limit_bytes=VMEM_LIMIT),
        name="layer_tail",
    )(x2d, pool_o, ssd_o, sb_o, p2d, wout, g1, b1, wup, fcw, fcb, wdown, wgate, wproj, g2, b2)


def kernel(x, p, w_in, pool_w, pool_scale, ssd_conv_w, ssd_conv_b, ssd_dt_bias, ssd_a_log, ssd_d,
           ssd_norm_w, w_out, ln1_g, ln1_b, ffn_w_up, ffn_conv_w, ffn_conv_b, ffn_w_down, ln2_g, ln2_b,
           ple_w_gate, ple_w_proj):
    batch, seq, _ = x.shape
    m = batch * seq
    x2d = x.reshape(m, D_MODEL)
    pad6 = lambda a: jnp.pad(a, (0, DT_PAD - SSD_HEADS)).reshape(1, DT_PAD)
    dt_cols = jnp.pad(w_in[:, :, COL_DT:COL_Q], ((0, 0), (0, 0), (0, DT_PAD - SSD_HEADS)))
    w_in_r = jnp.concatenate([w_in[:, :, :COL_DT], dt_cols, w_in[:, :, COL_Q:]], axis=2).astype(BF16)
    w_out_b, w_up_b, w_down_b = w_out.astype(BF16), ffn_w_up.astype(BF16), ffn_w_down.astype(BF16)
    w_gate_b, w_proj_b = ple_w_gate.astype(BF16), ple_w_proj.astype(BF16)
    p2d = p.reshape(DEPTH * m, PLE_DIM)
    for i in range(DEPTH):
        u, z, xbc, dt, q, k, v = _in_proj(x2d, w_in_r, i)

        w_bd = jax.scipy.linalg.block_diag(*[pool_w[i, g] for g in range(POOL_GROUPS)]).astype(BF16)
        pool_o = _pool(u, w_bd, pool_scale[i].reshape(1, POOL_WIDTH), batch, seq)

        ssd_o = _ssd(z, xbc, dt, ssd_conv_w[i], ssd_conv_b[i].reshape(1, -1), pad6(ssd_dt_bias[i]),
                     pad6(ssd_a_log[i]), jnp.repeat(ssd_d[i], HEAD_DIM).reshape(1, SSD_WIDTH),
                     ssd_norm_w[i].reshape(1, SSD_WIDTH), batch, seq)

        sb_o = _sb_attention(q, k, v, batch, seq)

        x2d = _tail(x2d, pool_o, ssd_o, sb_o, p2d, w_out_b, ln1_g[i].reshape(1, -1), ln1_b[i].reshape(1, -1),
                    w_up_b, ffn_conv_w[i], ffn_conv_b[i].reshape(1, -1), w_down_b, w_gate_b, w_proj_b,
                    ln2_g[i].reshape(1, -1), ln2_b[i].reshape(1, -1), seq, i)
    return x2d.reshape(batch, seq, D_MODEL)
```

```python
import functools

import jax
import jax.numpy as jnp
from jax import lax
from jax.experimental import pallas as pl
from jax.experimental.pallas import tpu as pltpu

F32 = jnp.float32
BF16 = jnp.bfloat16

D_MODEL = 1024
DEPTH = 4
PLE_DIM = 256
DEEPNORM_ALPHA = (2 * DEPTH) ** 0.25
LN_EPS = 1e-5
RMS_EPS = 1e-6
HEAD_DIM = 64
POOL_WIDTH = 256
POOL_GROUPS = 4
POOL_GROUP_DIM = 64
POOL_WINDOWS = (2, 4, 8, 16)
POOL_HALO = 16
SSD_WIDTH = 384
SSD_HEADS = 6
SSD_GROUPS = 2
SSD_GROUP_WIDTH = SSD_WIDTH // SSD_GROUPS
SSD_STATE = 128
SSD_CONV = 4
SSD_XBC_WIDTH = 896
SB_WIDTH = 384
D_FF = 2816
FFN_CONV = 3
COL_Z = 256
COL_XBC = COL_Z + SSD_WIDTH
COL_DT = COL_XBC + SSD_XBC_WIDTH
COL_Q = COL_DT + SSD_HEADS
COL_K = COL_Q + SB_WIDTH
COL_V = COL_K + SB_WIDTH

LANES = 128
SUBLANES = 8
DT_PAD = LANES
VMEM_LIMIT = 56 * 1024 * 1024

IN_TM = 1024
SSD_L = 256
SB_T = 256
SB_UNDERFLOW = 105.0
TAIL_TM = 256
FF_CHUNK = 256
N_FF_CHUNKS = D_FF // FF_CHUNK

IN_GROUPS = (("pool", POOL_WIDTH, BF16), ("z", SSD_WIDTH, BF16), ("xbc", SSD_XBC_WIDTH, BF16),
             ("dt", DT_PAD, F32), ("q", SB_WIDTH, BF16), ("k", SB_WIDTH, BF16), ("v", SB_WIDTH, BF16))
IN_COLS_PADDED = sum(w for _, w, _ in IN_GROUPS)


def _dot(a, b):
    return jnp.dot(a, b, preferred_element_type=F32)


def _dot_nt(a, b):
    return lax.dot_general(a, b, (((1,), (1,)), ((), ())), preferred_element_type=F32)


def _split3(x):
    hi = x.astype(BF16)
    r1 = x - hi.astype(F32)
    mid = r1.astype(BF16)
    lo = (r1 - mid.astype(F32)).astype(BF16)
    return hi, mid, lo


def _dot_sel_rhs(x, sel):
    hi, mid, lo = _split3(x)
    return _dot(hi, sel) + _dot(mid, sel) + _dot(lo, sel)


def _dot_sel_lhs(sel, x):
    hi, mid, lo = _split3(x)
    return _dot(sel, hi) + _dot(sel, mid) + _dot(sel, lo)


def _sigmoid(x):
    return 1.0 / (1.0 + jnp.exp(-x))


def _softplus(x):
    return jnp.maximum(x, 0.0) + jnp.log(1.0 + jnp.exp(-jnp.abs(x)))


def _layer_norm(y, g, b):
    mu = jnp.mean(y, axis=-1, keepdims=True)
    yc = y - mu
    var = jnp.mean(yc * yc, axis=-1, keepdims=True)
    return yc * lax.rsqrt(var + LN_EPS) * g + b


def _shift_rows(prev_rows, x, d):
    n = x.shape[0]
    cat = jnp.concatenate([prev_rows, x], axis=0)
    return pltpu.roll(cat, d, axis=0)[SUBLANES:SUBLANES + n]


def _const_spec(shape):
    zeros = (0,) * len(shape)
    return pl.BlockSpec(shape, lambda *_: zeros, pipeline_mode=pl.Buffered(1))


def _layer_spec(shape, layer):
    index = (layer,) + (0,) * len(shape)
    return pl.BlockSpec((None,) + tuple(shape), lambda *_: index, pipeline_mode=pl.Buffered(1))


def _in_proj_kernel(x_ref, w_ref, wpool_ref, pscale_ref, *refs, tiles_per_seq):
    out_refs, uprev_ref = refs[:-1], refs[-1]
    tm = IN_TM
    i = pl.program_id(0)

    @pl.when(i % tiles_per_seq == 0)
    def _():
        uprev_ref[...] = jnp.zeros_like(uprev_ref)

    h = _dot(x_ref[...].astype(BF16), w_ref[...])
    col = 0
    for (name, width, dtype), o_ref in zip(IN_GROUPS, out_refs):
        if name != "pool":
            o_ref[...] = h[:, col:col + width].astype(dtype)
        col += width

    u = h[:, :POOL_WIDTH]
    cat = jnp.concatenate([uprev_ref[...], u], axis=0)
    uprev_ref[...] = u[tm - POOL_HALO:, :]
    s2 = cat + pltpu.roll(cat, 1, axis=0)
    s4 = s2 + pltpu.roll(s2, 2, axis=0)
    s8 = s4 + pltpu.roll(s4, 4, axis=0)
    s16 = s8 + pltpu.roll(s8, 8, axis=0)
    s2, s4, s8, s16 = (s[POOL_HALO:, :] for s in (s2, s4, s8, s16))
    row = lax.broadcasted_iota(jnp.int32, u.shape, 0) + (i % tiles_per_seq) * tm
    group = lax.broadcasted_iota(jnp.int32, u.shape, 1) // POOL_GROUP_DIM
    win_sum = jnp.where(group == 0, s2, jnp.where(group == 1, s4, jnp.where(group == 2, s8, s16)))
    win = jnp.where(group == 0, 2, jnp.where(group == 1, 4, jnp.where(group == 2, 8, 16)))
    count = jnp.minimum(row + 1, win).astype(F32)
    pooled = win_sum / count - u
    out_refs[0][...] = (_dot(pooled.astype(BF16), wpool_ref[...]) * pscale_ref[...]).astype(BF16)


def _in_proj(x2d, w_in_r, w_pool_bd, pool_scale, layer, seq):
    m = x2d.shape[0]
    return pl.pallas_call(
        functools.partial(_in_proj_kernel, tiles_per_seq=seq // IN_TM),
        grid=(m // IN_TM,),
        in_specs=[pl.BlockSpec((IN_TM, D_MODEL), lambda i: (i, 0)),
                  _layer_spec((D_MODEL, IN_COLS_PADDED), layer),
                  _const_spec((POOL_WIDTH, POOL_WIDTH)),
                  _const_spec((1, POOL_WIDTH))],
        out_specs=[pl.BlockSpec((IN_TM, w), lambda i: (i, 0)) for _, w, _ in IN_GROUPS],
        out_shape=[jax.ShapeDtypeStruct((m, w), dt) for _, w, dt in IN_GROUPS],
        scratch_shapes=[pltpu.VMEM((POOL_HALO, POOL_WIDTH), F32)],
        compiler_params=pltpu.CompilerParams(dimension_semantics=("arbitrary",),
                                             vmem_limit_bytes=VMEM_LIMIT),
        name="in_proj_pool",
    )(x2d, w_in_r, w_pool_bd, pool_scale)


def _ssd_kernel(z_ref, xbc_ref, dt_ref, cw_ref, cb_ref, dtb_ref, alog_ref, dskip_ref, nw_ref,
                o_ref, tail_ref, state_ref):
    L = SSD_L

    @pl.when(pl.program_id(1) == 0)
    def _():
        tail_ref[...] = jnp.zeros_like(tail_ref)
        state_ref[...] = jnp.zeros_like(state_ref)

    x = xbc_ref[...].astype(F32)
    prev = tail_ref[...]
    conv = cb_ref[...] + cw_ref[3:4, :] * x
    for d in (1, 2, 3):
        conv = conv + cw_ref[3 - d:4 - d, :] * _shift_rows(prev, x, d)
    tail_ref[...] = x[L - SUBLANES:, :]
    xc = conv * _sigmoid(conv)
    xs = xc[:, :SSD_WIDTH]

    dt = _softplus(dt_ref[...] + dtb_ref[...])
    adt = dt * (-jnp.exp(alog_ref[...]))
    r_ll = lax.broadcasted_iota(jnp.int32, (L, L), 0)
    c_ll = lax.broadcasted_iota(jnp.int32, (L, L), 1)
    causal = c_ll <= r_ll
    lower = jnp.where(causal, 1.0, 0.0).astype(BF16)
    a_cum = _dot_sel_lhs(lower, adt)
    a_cum_t = a_cum.T

    e_r = lax.broadcasted_iota(jnp.int32, (DT_PAD, SSD_WIDTH), 0)
    e_c = lax.broadcasted_iota(jnp.int32, (DT_PAD, SSD_WIDTH), 1)
    expand = jnp.where(e_c // HEAD_DIM == e_r, 1.0, 0.0).astype(BF16)
    dt_e = _dot_sel_rhs(dt, expand)
    a_cum_e = _dot_sel_rhs(a_cum, expand)
    a_tot_e = a_cum_e[L - 1:L, :]

    xdt = xs * dt_e
    xdt_b = xdt.astype(BF16)
    xw_b = (xdt * jnp.exp(a_tot_e - a_cum_e)).astype(BF16)
    lane_w = lax.broadcasted_iota(jnp.int32, (L, SSD_WIDTH), 1)
    state = state_ref[...]
    state_b = state.astype(BF16)

    heads = []
    y_off = None
    new_state = None
    for g in range(SSD_GROUPS):
        b_off = SSD_WIDTH + g * SSD_STATE
        c_off = SSD_WIDTH + SSD_GROUPS * SSD_STATE + g * SSD_STATE
        bm = xc[:, b_off:b_off + SSD_STATE]
        cm_b = xc[:, c_off:c_off + SSD_STATE].astype(BF16)
        cb = _dot_nt(cm_b, bm.astype(BF16))
        for r in range(SSD_HEADS // SSD_GROUPS):
            h = g * (SSD_HEADS // SSD_GROUPS) + r
            seg = a_cum[:, h:h + 1] - a_cum_t[h:h + 1, :]
            decay = jnp.exp(jnp.where(causal, seg, -1e30))
            wmat = (cb * decay).astype(BF16)
            pair = (h // 2) * LANES
            heads.append(_dot(wmat, xdt_b[:, pair:pair + LANES]))
        st_g = _dot(bm.T.astype(BF16), xw_b)
        yo_g = _dot(cm_b, state_b)
        if g == 0:
            new_state, y_off = st_g, yo_g
        else:
            in_g0 = lax.broadcasted_iota(jnp.int32, st_g.shape, 1) < SSD_GROUP_WIDTH
            new_state = jnp.where(in_g0, new_state, st_g)
            y_off = jnp.where(lane_w < SSD_GROUP_WIDTH, y_off, yo_g)

    state_ref[...] = state * jnp.exp(a_tot_e) + new_state

    lane_p = lax.broadcasted_iota(jnp.int32, (L, LANES), 1)
    y_diag = jnp.concatenate(
        [jnp.where(lane_p < HEAD_DIM, heads[2 * p], heads[2 * p + 1]) for p in range(SSD_HEADS // 2)],
        axis=1)
    y = y_diag + y_off * jnp.exp(a_cum_e) + xs * dskip_ref[...]

    zf = z_ref[...].astype(F32)
    hg = y * (zf * _sigmoid(zf))
    sq = hg * hg
    in_g0 = lane_w < SSD_GROUP_WIDTH
    ms0 = jnp.sum(jnp.where(in_g0, sq, 0.0), axis=-1, keepdims=True) * (1.0 / SSD_GROUP_WIDTH)
    ms1 = jnp.sum(jnp.where(in_g0, 0.0, sq), axis=-1, keepdims=True) * (1.0 / SSD_GROUP_WIDTH)
    inv = jnp.where(in_g0, lax.rsqrt(ms0 + RMS_EPS), lax.rsqrt(ms1 + RMS_EPS))
    o_ref[...] = (hg * inv * nw_ref[...]).astype(o_ref.dtype)


def _ssd(z2d, xbc2d, dt2d, cw, cb, dtb, alog, dskip, nw, batch, seq):
    m = z2d.shape[0]
    nc = seq // SSD_L
    row_map = lambda b, c: (b * nc + c, 0)
    return pl.pallas_call(
        _ssd_kernel,
        grid=(batch, nc),
        in_specs=[pl.BlockSpec((SSD_L, SSD_WIDTH), row_map),
                  pl.BlockSpec((SSD_L, SSD_XBC_WIDTH), row_map),
                  pl.BlockSpec((SSD_L, DT_PAD), row_map),
                  _const_spec((SSD_CONV, SSD_XBC_WIDTH)),
                  _const_spec((1, SSD_XBC_WIDTH)),
                  _const_spec((1, DT_PAD)),
                  _const_spec((1, DT_PAD)),
                  _const_spec((1, SSD_WIDTH)),
                  _const_spec((1, SSD_WIDTH))],
        out_specs=pl.BlockSpec((SSD_L, SSD_WIDTH), row_map),
        out_shape=jax.ShapeDtypeStruct((m, SSD_WIDTH), BF16),
        scratch_shapes=[pltpu.VMEM((SUBLANES, SSD_XBC_WIDTH), F32),
                        pltpu.VMEM((SSD_STATE, SSD_WIDTH), F32)],
        compiler_params=pltpu.CompilerParams(dimension_semantics=("parallel", "arbitrary"),
                                             vmem_limit_bytes=VMEM_LIMIT),
        name="ssd_mixer",
    )(z2d, xbc2d, dt2d, cw, cb, dtb, alog, dskip, nw)


def _sb_kernel(q_ref, k_ref, v_ref, o_ref):
    T = SB_T
    nq = q_ref.shape[0] // T
    lane = lax.broadcasted_iota(jnp.int32, (T, LANES), 1)
    r_tt = lax.broadcasted_iota(jnp.int32, (T, T), 0)
    c_tt = lax.broadcasted_iota(jnp.int32, (T, T), 1)
    strict = c_tt < r_tt
    later = jnp.where(r_tt > c_tt, 1.0, 0.0).astype(BF16)
    head_lanes = [(lane >= HEAD_DIM) == bool(hh) for hh in range(2)]

    def rows(ref, j):
        return ref[pl.ds(pl.multiple_of(j * T, T), T), :]

    def head_queries(qi):
        qs = rows(q_ref, qi) * (HEAD_DIM ** -0.5)
        return [jnp.where(m, qs, jnp.zeros_like(qs)) for m in head_lanes]

    def tile_terms(z, diagonal):
        sp = _softplus(z)
        spm = jnp.where(strict, sp, 0.0) if diagonal else sp
        right = _dot(spm.astype(BF16), later)
        return z - sp, right, right[:, 0:1] + spm[:, 0:1]

    def tile_out(logsig, right, carry, vj, diagonal):
        w = jnp.exp(logsig - right if carry is None else logsig - right - carry)
        if diagonal:
            w = jnp.where(strict, w, 0.0)
        return _dot(w.astype(BF16), vj)

    def store(qi, accs):
        o_ref[pl.ds(pl.multiple_of(qi * T, T), T), :] = jnp.where(
            lane < HEAD_DIM, accs[0], accs[1]).astype(o_ref.dtype)

    def sweep_rest(qi, qhs, carries, accs):
        def cond(st):
            return jnp.logical_and(st[0] >= 0, st[1] < SB_UNDERFLOW)

        def body(st):
            j, _, c0, a0, c1, a1 = st
            kj, vj = rows(k_ref, j), rows(v_ref, j)
            terms = [tile_terms(z, False) for z in [_dot_nt(qh, kj) for qh in qhs]]
            new = [(c + tot, a + tile_out(ls, right, c, vj, False))
                   for (ls, right, tot), c, a in zip(terms, (c0, c1), (a0, a1))]
            cmin = jnp.min(jnp.minimum(new[0][0], new[1][0]))
            return j - 1, cmin, new[0][0], new[0][1], new[1][0], new[1][1]

        cmin0 = jnp.min(jnp.minimum(carries[0], carries[1]))
        st = lax.while_loop(cond, body, (qi - 2, cmin0, carries[0], accs[0], carries[1], accs[1]))
        return st[3], st[5]

    def q_tiles(tiles):
        chains = []
        qhs = [head_queries(qi) for qi, _ in tiles]
        for t, (qi, has_left) in enumerate(tiles):
            for hh in range(2):
                chains.append((t, hh, True, qi))
                if has_left:
                    chains.append((t, hh, False, qi - 1))
        zs = [_dot_nt(qhs[t][hh], rows(k_ref, kt)) for t, hh, _, kt in chains]
        terms = [tile_terms(z, diag) for z, (_, _, diag, _) in zip(zs, chains)]
        accs = [[None, None] for _ in tiles]
        carries = [[None, None] for _ in tiles]
        for (t, hh, diag, kt), (ls, right, tot) in zip(chains, terms):
            out = tile_out(ls, right, None if diag else carries[t][hh], rows(v_ref, kt), diag)
            accs[t][hh] = out if diag else accs[t][hh] + out
            carries[t][hh] = tot if diag else carries[t][hh] + tot
        for t, (qi, has_left) in enumerate(tiles):
            store(qi, sweep_rest(qi, qhs[t], carries[t], accs[t]) if has_left else accs[t])

    q_tiles([(0, False), (1, True)])

    def pair(i, _):
        q_tiles([(2 * i, True), (2 * i + 1, True)])
        return 0

    lax.fori_loop(1, nq // 2, pair, 0)


def _sb_attention(q2d, k2d, v2d, batch, seq):
    m = q2d.shape[0]
    pairs = SB_WIDTH // LANES
    blk = pl.BlockSpec((seq, LANES), lambda b, p: (b, p))
    return pl.pallas_call(
        _sb_kernel,
        grid=(batch, pairs),
        in_specs=[blk, blk, blk],
        out_specs=blk,
        out_shape=jax.ShapeDtypeStruct((m, SB_WIDTH), BF16),
        compiler_params=pltpu.CompilerParams(dimension_semantics=("parallel", "parallel"),
                                             vmem_limit_bytes=VMEM_LIMIT),
        name="sb_attention",
    )(q2d, k2d, v2d)


def _tail_kernel(x_ref, pool_ref, ssd_ref, sb_ref, p_ref, wout_ref, g1_ref, b1_ref,
                 wup_ref, fcw_ref, fcb_ref, wdown_ref, wgate_ref, wproj_ref, g2_ref, b2_ref,
                 o_ref, carry_ref, acc_ref, *, tiles_per_seq):
    tm = TAIL_TM
    j = pl.program_id(0)

    @pl.when(j == 0)
    def _():
        acc_ref[...] = jnp.zeros_like(acc_ref)

    @pl.when(j % tiles_per_seq == 0)
    def _():
        carry_ref[...] = jnp.zeros_like(carry_ref)

    o_ref[...] = _layer_norm(acc_ref[...], g2_ref[...], b2_ref[...])

    mix = _dot(jnp.concatenate([pool_ref[...], ssd_ref[...], sb_ref[...]], axis=1), wout_ref[...])
    ple = _dot(p_ref[...].astype(BF16), wproj_ref[...])
    x1 = _layer_norm(DEEPNORM_ALPHA * x_ref[...] + mix, g1_ref[...], b1_ref[...])
    xb = x1.astype(BF16)

    gate = _sigmoid(_dot(xb, wgate_ref[...]))
    acc_ref[...] = DEEPNORM_ALPHA * x1 + gate * ple

    def chunk_cols(ref, c):
        g0, v0 = c * FF_CHUNK, D_FF + c * FF_CHUNK
        return jnp.concatenate([ref[:, g0:g0 + FF_CHUNK], ref[:, v0:v0 + FF_CHUNK]], axis=1)

    pending = [_dot(xb, chunk_cols(wup_ref, c)) for c in range(2)]
    for c in range(N_FF_CHUNKS):
        up = pending.pop(0)
        if c + 2 < N_FF_CHUNKS:
            pending.append(_dot(xb, chunk_cols(wup_ref, c + 2)))
        prev = carry_ref[c]
        cw = chunk_cols(fcw_ref, c)
        conv = (chunk_cols(fcb_ref, c) + cw[2:3, :] * up
                + cw[1:2, :] * _shift_rows(prev, up, 1)
                + cw[0:1, :] * _shift_rows(prev, up, 2))
        carry_ref[c] = up[tm - SUBLANES:, :]
        gl = conv[:, :FF_CHUNK]
        act = (gl * _sigmoid(gl) * conv[:, FF_CHUNK:]).astype(BF16)
        acc_ref[...] += _dot(act, wdown_ref[c * FF_CHUNK:(c + 1) * FF_CHUNK, :])


def _tail(x2d, pool_o, ssd_o, sb_o, p2d, wout, g1, b1, wup, fcw, fcb, wdown, wgate, wproj, g2, b2, seq,
          layer):
    m = x2d.shape[0]
    nt = m // TAIL_TM
    cur_map = lambda j: (jnp.minimum(j, nt - 1), 0)
    p_map = lambda j: (layer * nt + jnp.minimum(j, nt - 1), 0)
    prev_map = lambda j: (jnp.maximum(j - 1, 0), 0)
    return pl.pallas_call(
        functools.partial(_tail_kernel, tiles_per_seq=seq // TAIL_TM),
        grid=(nt + 1,),
        in_specs=[pl.BlockSpec((TAIL_TM, D_MODEL), cur_map),
                  pl.BlockSpec((TAIL_TM, POOL_WIDTH), cur_map),
                  pl.BlockSpec((TAIL_TM, SSD_WIDTH), cur_map),
                  pl.BlockSpec((TAIL_TM, SB_WIDTH), cur_map),
                  pl.BlockSpec((TAIL_TM, PLE_DIM), p_map),
                  _layer_spec((D_MODEL, D_MODEL), layer),
                  _const_spec((1, D_MODEL)),
                  _const_spec((1, D_MODEL)),
                  _layer_spec((D_MODEL, 2 * D_FF), layer),
                  _const_spec((FFN_CONV, 2 * D_FF)),
                  _const_spec((1, 2 * D_FF)),
                  _layer_spec((D_FF, D_MODEL), layer),
                  _layer_spec((D_MODEL, D_MODEL), layer),
                  _layer_spec((PLE_DIM, D_MODEL), layer),
                  _const_spec((1, D_MODEL)),
                  _const_spec((1, D_MODEL))],
        out_specs=pl.BlockSpec((TAIL_TM, D_MODEL), prev_map),
        out_shape=jax.ShapeDtypeStruct((m, D_MODEL), F32),
        scratch_shapes=[pltpu.VMEM((N_FF_CHUNKS, SUBLANES, 2 * FF_CHUNK), F32),
                        pltpu.VMEM((TAIL_TM, D_MODEL), F32)],
        compiler_params=pltpu.CompilerParams(dimension_semantics=("arbitrary",),
                                             vmem_limit_bytes=VMEM_LIMIT),
        name="layer_tail",
    )(x2d, pool_o, ssd_o, sb_o, p2d, wout, g1, b1, wup, fcw, fcb, wdown, wgate, wproj, g2, b2)


def kernel(x, p, w_in, pool_w, pool_scale, ssd_conv_w, ssd_conv_b, ssd_dt_bias, ssd_a_log, ssd_d,
           ssd_norm_w, w_out, ln1_g, ln1_b, ffn_w_up, ffn_conv_w, ffn_conv_b, ffn_w_down, ln2_g, ln2_b,
           ple_w_gate, ple_w_proj):
    batch, seq, d_model = x.shape
    assert d_model == D_MODEL and p.shape == (DEPTH, batch, seq, PLE_DIM)
    assert all(seq % t == 0 for t in (IN_TM, SSD_L, 2 * SB_T, TAIL_TM))
    m = batch * seq
    x2d = x.reshape(m, D_MODEL)
    pad6 = lambda a: jnp.pad(a, (0, DT_PAD - SSD_HEADS)).reshape(1, DT_PAD)
    dt_cols = jnp.pad(w_in[:, :, COL_DT:COL_Q], ((0, 0), (0, 0), (0, DT_PAD - SSD_HEADS)))
    w_in_r = jnp.concatenate([w_in[:, :, :COL_DT], dt_cols, w_in[:, :, COL_Q:]], axis=2).astype(BF16)
    w_out_b, w_up_b, w_down_b = w_out.astype(BF16), ffn_w_up.astype(BF16), ffn_w_down.astype(BF16)
    w_gate_b, w_proj_b = ple_w_gate.astype(BF16), ple_w_proj.astype(BF16)
    p2d = p.reshape(DEPTH * m, PLE_DIM)
    for i in range(DEPTH):
        w_bd = jax.scipy.linalg.block_diag(*[pool_w[i, g] for g in range(POOL_GROUPS)]).astype(BF16)
        pool_o, z, xbc, dt, q, k, v = _in_proj(x2d, w_in_r, w_bd, pool_scale[i].reshape(1, POOL_WIDTH), i, seq)

        ssd_o = _ssd(z, xbc, dt, ssd_conv_w[i], ssd_conv_b[i].reshape(1, -1), pad6(ssd_dt_bias[i]),
                     pad6(ssd_a_log[i]), jnp.repeat(ssd_d[i], HEAD_DIM).reshape(1, SSD_WIDTH),
                     ssd_norm_w[i].reshape(1, SSD_WIDTH), batch, seq)

        sb_o = _sb_attention(q, k, v, batch, seq)

        x2d = _tail(x2d, pool_o, ssd_o, sb_o, p2d, w_out_b, ln1_g[i].reshape(1, -1), ln1_b[i].reshape(1, -1),
                    w_up_b, ffn_conv_w[i], ffn_conv_b[i].reshape(1, -1), w_down_b, w_gate_b, w_proj_b,
                    ln2_g[i].reshape(1, -1), ln2_b[i].reshape(1, -1), seq, i)
    return x2d.reshape(batch, seq, D_MODEL)
```
